```python
import math
import jax, jax.numpy as jnp
from jax import lax
import numpy as np

D_MODEL = 2048
BATCH = 2
SEQ = 16384
DEPTH = 2
DEC_BATCH = 32
DEC_SEQ = 64
PAST_LEN = 1024

CHUNK = 64
N_REC_LAYERS = (DEPTH + 1) // 2
N_ATT_LAYERS = DEPTH // 2
RMS_EPS = 1e-6

LRU_WIDTH = D_MODEL // 2
LRU_HEADS = 8
LRU_HEAD_DIM = LRU_WIDTH // LRU_HEADS
CONV_WIDTH = 4
LRU_C = 8.0
POOL_WIDTH = D_MODEL // 2
POOL_WINDOWS = (2, 4, 8, 16)
POOL_GROUPS = len(POOL_WINDOWS)
POOL_GROUP_DIM = POOL_WIDTH // POOL_GROUPS
POOL_MAX = 16
MIX_IN = 2 * LRU_WIDTH + POOL_WIDTH
MIX_OUT = LRU_WIDTH + POOL_WIDTH

MLA_HEADS = 16
Q_LORA = 512
KV_LORA = 256
NOPE_DIM = 128
ROPE_DIM = 64
V_DIM = 128
QK_DIM = NOPE_DIM + ROPE_DIM
ROPE_THETA = 10000.0
Q_BLOCK = 128

PEER_HEADS = 8
N_KEYS = 128
N_EXPERTS = N_KEYS * N_KEYS
PEER_TOPK = 16
D_KEY = 256
HALF_KEY = D_KEY // 2
PEER_BLOCK = 128

kernel_name = 'hybrid_rglru_pool_mla_peer_stream_step'


def rmsnorm(x, g):
    xf = x.astype(jnp.float32)
    y = xf * lax.rsqrt(jnp.mean(xf * xf, axis=-1, keepdims=True) + RMS_EPS)
    return (y * g.astype(jnp.float32)).astype(x.dtype)


def modulate(h, shift, scale):
    return h * (1.0 + scale[:, None, :]) + shift[:, None, :]


def causal_dwconv(u, prev, w, b):
    t = u.shape[1]
    z = jnp.concatenate([prev.astype(u.dtype), u], axis=1)
    out = b + sum(z[:, k:k + t, :] * w[k] for k in range(CONV_WIDTH))
    return out, z[:, -(CONV_WIDTH - 1):, :]


def _lin_combine(left, right):
    a_l, b_l = left
    a_r, b_r = right
    return a_r * a_l, a_r * b_l + b_r


def rg_lru(xc, h_prev, pos, w_a, b_a, w_x, b_x, lam):
    bsz, t, _ = xc.shape
    xf = xc.astype(jnp.float32)
    xh = xf.reshape(bsz, t, LRU_HEADS, LRU_HEAD_DIM)
    r = jax.nn.sigmoid(jnp.einsum('bthi,hij->bthj', xh, w_a.astype(jnp.float32)).reshape(bsz, t, LRU_WIDTH)
                       + b_a.astype(jnp.float32))
    i = jax.nn.sigmoid(jnp.einsum('bthi,hij->bthj', xh, w_x.astype(jnp.float32)).reshape(bsz, t, LRU_WIDTH)
                       + b_x.astype(jnp.float32))
    log_a = -LRU_C * r * jax.nn.softplus(-lam.astype(jnp.float32))
    a = jnp.exp(log_a)
    mult = jnp.sqrt(-jnp.expm1(2.0 * log_a))
    mult = jnp.where((pos == 0)[None, :, None], 1.0, mult)
    u = mult * (i * xf)
    u = u.at[:, 0].add(a[:, 0] * h_prev.astype(jnp.float32))
    _, h = lax.associative_scan(_lin_combine, (a, u), axis=1)
    return h, h[:, -1]


def pool_mix(u, prev, pos, w_pool, s_pool):
    bsz, t, c = u.shape
    p_len = POOL_MAX - 1
    z = jnp.concatenate([prev.astype(u.dtype), u], axis=1)
    zf = z.astype(jnp.float32)
    cs = jnp.cumsum(jnp.pad(zf, ((0, 0), (1, 0), (0, 0))), axis=1)
    end = cs[:, p_len + 1:]
    means = []
    for g, w in enumerate(POOL_WINDOWS):
        sl = slice(g * POOL_GROUP_DIM, (g + 1) * POOL_GROUP_DIM)
        start = cs[:, p_len + 1 - w:p_len + 1 - w + t, sl]
        cnt = jnp.minimum(pos + 1, w).astype(jnp.float32)[None, :, None]
        means.append((end[..., sl] - start) / cnt)
    pooled = jnp.concatenate(means, axis=-1) - u.astype(jnp.float32)
    pooled = pooled.astype(u.dtype).reshape(bsz, t, POOL_GROUPS, POOL_GROUP_DIM)
    out = jnp.einsum('btgi,gij->btgj', pooled, w_pool).reshape(bsz, t, c) * s_pool
    return out, z[:, -p_len:, :]


def apply_rope(x, pos):
    half = ROPE_DIM // 2
    inv = ROPE_THETA ** (-(jnp.arange(half, dtype=jnp.float32) / half))
    ang = pos.astype(jnp.float32)[:, None] * inv[None, :]
    shape = (1, pos.shape[0]) + (1,) * (x.ndim - 3) + (half,)
    cos = jnp.cos(ang).reshape(shape)
    sin = jnp.sin(ang).reshape(shape)
    x1 = x[..., :half].astype(jnp.float32)
    x2 = x[..., half:].astype(jnp.float32)
    return jnp.concatenate([x1 * cos - x2 * sin, x1 * sin + x2 * cos], axis=-1).astype(x.dtype)


def mla_project(h, pos, w_dq, g_q, w_uq, w_dkv, g_kv):
    bsz, t, _ = h.shape
    cq = rmsnorm(h @ w_dq, g_q)
    q = (cq @ w_uq).reshape(bsz, t, MLA_HEADS, QK_DIM)
    q_nope = q[..., :NOPE_DIM]
    q_pe = apply_rope(q[..., NOPE_DIM:], pos)
    kv = h @ w_dkv
    c_kv = rmsnorm(kv[..., :KV_LORA], g_kv)
    k_pe = apply_rope(kv[..., KV_LORA:], pos)
    return q_nope, q_pe, c_kv, k_pe


def mla_attend(q_nope, q_pe, q_pos, c_kv, k_pe, k_pos, w_ukv):
    bsz, tk, _ = c_kv.shape
    tq = q_nope.shape[1]
    kv = (c_kv @ w_ukv).reshape(bsz, tk, MLA_HEADS, NOPE_DIM + V_DIM)
    k_nope = kv[..., :NOPE_DIM]
    v = kv[..., NOPE_DIM:]
    k_chunk = k_pos // CHUNK
    scale = QK_DIM ** -0.5

    def block(args):
        qn, qp, qpos = args
        s = (jnp.einsum('bqhd,bkhd->bhqk', qn, k_nope) + jnp.einsum('bqhr,bkr->bhqk', qp, k_pe))
        s = s.astype(jnp.float32) * scale
        mask = k_chunk[None, :] <= (qpos // CHUNK)[:, None]
        s = jnp.where(mask[None, None], s, -jnp.inf)
        p = jax.nn.softmax(s, axis=-1).astype(v.dtype)
        return jnp.einsum('bhqk,bkhd->bqhd', p, v)

    if tq <= Q_BLOCK:
        return block((q_nope, q_pe, q_pos))
    nb = tq // Q_BLOCK

    def split(a):
        return jnp.moveaxis(a.reshape((a.shape[0], nb, Q_BLOCK) + a.shape[2:]), 1, 0)

    o = lax.map(block, (split(q_nope), split(q_pe), q_pos.reshape(nb, Q_BLOCK)))
    return jnp.moveaxis(o, 0, 1).reshape(bsz, tq, MLA_HEADS, V_DIM)


def peer_ffn(h, w_q, sub_keys, u_emb, v_emb):
    bsz, t, d = h.shape
    m = bsz * t
    hf = h.reshape(m, d)
    q = (hf @ w_q).reshape(m, PEER_HEADS, 2, HALF_KEY)
    s = jnp.einsum('mhpd,hpnd->mhpn', q, sub_keys).astype(jnp.float32)
    s1, i1 = lax.top_k(s[:, :, 0], PEER_TOPK)
    s2, i2 = lax.top_k(s[:, :, 1], PEER_TOPK)
    cand_s = (s1[..., :, None] + s2[..., None, :]).reshape(m, PEER_HEADS, PEER_TOPK * PEER_TOPK)
    cand_i = (i1[..., :, None] * N_KEYS + i2[..., None, :]).reshape(m, PEER_HEADS, PEER_TOPK * PEER_TOPK)
    top_s, sel = lax.top_k(cand_s, PEER_TOPK)
    idx = jnp.take_along_axis(cand_i, sel, axis=-1)
    gate = jax.nn.softmax(top_s, axis=-1)
    pad = (-m) % PEER_BLOCK
    nb = (m + pad) // PEER_BLOCK
    hp = jnp.pad(hf, ((0, pad), (0, 0))).reshape(nb, PEER_BLOCK, d)
    ip = jnp.pad(idx, ((0, pad), (0, 0), (0, 0))).reshape(nb, PEER_BLOCK, PEER_HEADS, PEER_TOPK)
    gp = jnp.pad(gate, ((0, pad), (0, 0), (0, 0))).reshape(nb, PEER_BLOCK, PEER_HEADS, PEER_TOPK)

    def block(args):
        xb, ib, gb = args
        act = jax.nn.gelu(jnp.einsum('pd,phkd->phk', xb, u_emb[ib]).astype(jnp.float32), approximate=False) * gb
        return jnp.einsum('phk,phkd->pd', act.astype(xb.dtype), v_emb[ib])

    out = lax.map(block, (hp, ip, gp))
    return out.reshape(nb * PEER_BLOCK, d)[:m].reshape(bsz, t, d)


def run_trunk(x, c, pos, past_pos, conv_prev, h_prev, pool_prev, ckv_prev, kpe_prev, w):
    bsz, t, _ = x.shape
    sc = jax.nn.silu(c.astype(jnp.float32)).astype(x.dtype)
    new_conv, new_h, new_pool, new_ckv, new_kpe = [], [], [], [], []
    for l in range(DEPTH):
        mod = sc @ w['ada_w'][l] + w['ada_b'][l]
        sh1, sc1, g1, sh2, sc2, g2 = jnp.split(mod, 6, axis=-1)
        hn = modulate(rmsnorm(x, w['norm1_g'][l]), sh1, sc1)
        if l % 2 == 0:
            r = l // 2
            z = hn @ w['rec_w_in'][r]
            gate_br = z[..., :LRU_WIDTH]
            rec_br = z[..., LRU_WIDTH:2 * LRU_WIDTH]
            pool_in = z[..., 2 * LRU_WIDTH:]
            xc, conv_s = causal_dwconv(rec_br, conv_prev[r], w['rec_conv_w'][r], w['rec_conv_b'][r])
            hseq, h_last = rg_lru(xc, h_prev[r], pos, w['rec_w_a'][r], w['rec_b_a'][r],
                                  w['rec_w_x'][r], w['rec_b_x'][r], w['rec_lambda'][r])
            a_out = jax.nn.gelu(gate_br) * hseq.astype(x.dtype)
            b_out, pool_s = pool_mix(pool_in, pool_prev[r], pos, w['pool_w'][r], w['pool_scale'][r])
            mix = jnp.concatenate([a_out, b_out], axis=-1) @ w['rec_w_out'][r]
            new_conv.append(conv_s)
            new_h.append(h_last.astype(x.dtype))
            new_pool.append(pool_s)
        else:
            a = l // 2
            q_nope, q_pe, ckv, kpe = mla_project(hn, pos, w['att_w_dq'][a], w['att_g_q'][a], w['att_w_uq'][a],
                                                 w['att_w_dkv'][a], w['att_g_kv'][a])
            ckv_all = jnp.concatenate([ckv_prev[a].astype(ckv.dtype), ckv], axis=1)
            kpe_all = jnp.concatenate([kpe_prev[a].astype(kpe.dtype), kpe], axis=1)
            k_pos = jnp.concatenate([past_pos, pos])
            o = mla_attend(q_nope, q_pe, pos, ckv_all, kpe_all, k_pos, w['att_w_ukv'][a])
            mix = o.reshape(bsz, t, MLA_HEADS * V_DIM) @ w['att_w_out'][a]
            new_ckv.append(ckv)
            new_kpe.append(kpe)
        x = x + g1[:, None, :] * mix
        hn2 = modulate(rmsnorm(x, w['norm2_g'][l]), sh2, sc2)
        x = x + g2[:, None, :] * peer_ffn(hn2, w['peer_w_q'][l], w['peer_keys'][l], w['peer_u'][l], w['peer_v'][l])
    y = rmsnorm(x, w['final_g'])
    return (y, jnp.stack(new_conv), jnp.stack(new_h), jnp.stack(new_pool), jnp.stack(new_ckv), jnp.stack(new_kpe))


def setup_inputs(seed: int = 0) -> dict:
    key = jax.random.key(seed)
    ks = iter(jax.random.split(key, 48))
    f32 = jnp.float32

    def nrm(shape, s):
        return jax.random.normal(next(ks), shape, f32) * s

    def gain(shape):
        return 1.0 + nrm(shape, 0.02)

    a8 = jax.random.uniform(next(ks), (N_REC_LAYERS, LRU_WIDTH), f32, 0.9, 0.999)
    a_base = a8 ** (1.0 / LRU_C)
    rec_lambda = jnp.log(a_base) - jnp.log1p(-a_base)
    return {
        'x_prompt': nrm((BATCH, SEQ, D_MODEL), 1.0),
        'x_sample': nrm((DEC_BATCH, DEC_SEQ, D_MODEL), 1.0),
        'c_prompt': nrm((BATCH, D_MODEL), 1.0),
        'c_sample': nrm((DEC_BATCH, D_MODEL), 1.0),
        'state_conv': nrm((N_REC_LAYERS, DEC_BATCH, CONV_WIDTH - 1, LRU_WIDTH), 1.0),
        'state_lru_h': nrm((N_REC_LAYERS, DEC_BATCH, LRU_WIDTH), 0.5),
        'state_pool': nrm((N_REC_LAYERS, DEC_BATCH, POOL_MAX - 1, POOL_WIDTH), 1.0),
        'cache_ckv': nrm((N_ATT_LAYERS, DEC_BATCH, PAST_LEN, KV_LORA), 1.0),
        'cache_kpe': nrm((N_ATT_LAYERS, DEC_BATCH, PAST_LEN, ROPE_DIM), 1.0),
        'ada_w': nrm((DEPTH, D_MODEL, 6 * D_MODEL), 0.5 * D_MODEL ** -0.5),
        'ada_b': nrm((DEPTH, 6 * D_MODEL), 0.01),
        'norm1_g': gain((DEPTH, D_MODEL)),
        'norm2_g': gain((DEPTH, D_MODEL)),
        'rec_w_in': nrm((N_REC_LAYERS, D_MODEL, MIX_IN), D_MODEL ** -0.5),
        'rec_conv_w': nrm((N_REC_LAYERS, CONV_WIDTH, LRU_WIDTH), CONV_WIDTH ** -0.5),
        'rec_conv_b': nrm((N_REC_LAYERS, LRU_WIDTH), 0.01),
        'rec_w_a': nrm((N_REC_LAYERS, LRU_HEADS, LRU_HEAD_DIM, LRU_HEAD_DIM), LRU_HEAD_DIM ** -0.5),
        'rec_b_a': nrm((N_REC_LAYERS, LRU_WIDTH), 0.1),
        'rec_w_x': nrm((N_REC_LAYERS, LRU_HEADS, LRU_HEAD_DIM, LRU_HEAD_DIM), LRU_HEAD_DIM ** -0.5),
        'rec_b_x': nrm((N_REC_LAYERS, LRU_WIDTH), 0.1),
        'rec_lambda': rec_lambda,
        'pool_w': nrm((N_REC_LAYERS, POOL_GROUPS, POOL_GROUP_DIM, POOL_GROUP_DIM), POOL_GROUP_DIM ** -0.5),
        'pool_scale': 1.0 + nrm((N_REC_LAYERS, POOL_WIDTH), 0.1),
        'rec_w_out': nrm((N_REC_LAYERS, MIX_OUT, D_MODEL), MIX_OUT ** -0.5),
        'att_w_dq': nrm((N_ATT_LAYERS, D_MODEL, Q_LORA), D_MODEL ** -0.5),
        'att_g_q': gain((N_ATT_LAYERS, Q_LORA)),
        'att_w_uq': nrm((N_ATT_LAYERS, Q_LORA, MLA_HEADS * QK_DIM), Q_LORA ** -0.5),
        'att_w_dkv': nrm((N_ATT_LAYERS, D_MODEL, KV_LORA + ROPE_DIM), D_MODEL ** -0.5),
        'att_g_kv': gain((N_ATT_LAYERS, KV_LORA)),
        'att_w_ukv': nrm((N_ATT_LAYERS, KV_LORA, MLA_HEADS * (NOPE_DIM + V_DIM)), KV_LORA ** -0.5),
        'att_w_out': nrm((N_ATT_LAYERS, MLA_HEADS * V_DIM, D_MODEL), (MLA_HEADS * V_DIM) ** -0.5),
        'peer_w_q': nrm((DEPTH, D_MODEL, PEER_HEADS * D_KEY), D_MODEL ** -0.5),
        'peer_keys': nrm((DEPTH, PEER_HEADS, 2, N_KEYS, HALF_KEY), HALF_KEY ** -0.5),
        'peer_u': nrm((DEPTH, N_EXPERTS, D_MODEL), D_MODEL ** -0.5),
        'peer_v': nrm((DEPTH, N_EXPERTS, D_MODEL), 0.25),
        'final_g': gain((D_MODEL,)),
    }


def reference(x_prompt, x_sample, c_prompt, c_sample, state_conv, state_lru_h, state_pool, cache_ckv, cache_kpe,
              ada_w, ada_b, norm1_g, norm2_g, rec_w_in, rec_conv_w, rec_conv_b, rec_w_a, rec_b_a, rec_w_x, rec_b_x,
              rec_lambda, pool_w, pool_scale, rec_w_out, att_w_dq, att_g_q, att_w_uq, att_w_dkv, att_g_kv, att_w_ukv,
              att_w_out, peer_w_q, peer_keys, peer_u, peer_v, final_g):
    w = dict(ada_w=ada_w, ada_b=ada_b, norm1_g=norm1_g, norm2_g=norm2_g, rec_w_in=rec_w_in, rec_conv_w=rec_conv_w,
             rec_conv_b=rec_conv_b, rec_w_a=rec_w_a, rec_b_a=rec_b_a, rec_w_x=rec_w_x, rec_b_x=rec_b_x,
             rec_lambda=rec_lambda, pool_w=pool_w, pool_scale=pool_scale, rec_w_out=rec_w_out, att_w_dq=att_w_dq,
             att_g_q=att_g_q, att_w_uq=att_w_uq, att_w_dkv=att_w_dkv, att_g_kv=att_g_kv, att_w_ukv=att_w_ukv,
             att_w_out=att_w_out, peer_w_q=peer_w_q, peer_keys=peer_keys, peer_u=peer_u, peer_v=peer_v,
             final_g=final_g)
    dt = x_prompt.dtype
    bp, tp, _ = x_prompt.shape
    pos_p = jnp.arange(tp, dtype=jnp.int32)
    (y_prompt, new_conv_prompt, new_h_prompt, new_pool_prompt, new_ckv_prompt, new_kpe_prompt) = run_trunk(
        x_prompt, c_prompt, pos_p, jnp.zeros((0,), jnp.int32),
        jnp.zeros((N_REC_LAYERS, bp, CONV_WIDTH - 1, LRU_WIDTH), dt),
        jnp.zeros((N_REC_LAYERS, bp, LRU_WIDTH), dt),
        jnp.zeros((N_REC_LAYERS, bp, POOL_MAX - 1, POOL_WIDTH), dt),
        jnp.zeros((N_ATT_LAYERS, bp, 0, KV_LORA), dt),
        jnp.zeros((N_ATT_LAYERS, bp, 0, ROPE_DIM), dt), w)
    past_len = cache_ckv.shape[2]
    ts = x_sample.shape[1]
    pos_s = past_len + jnp.arange(ts, dtype=jnp.int32)
    (y_sample, new_conv_sample, new_h_sample, new_pool_sample, new_ckv_sample, new_kpe_sample) = run_trunk(
        x_sample, c_sample, pos_s, jnp.arange(past_len, dtype=jnp.int32),
        state_conv, state_lru_h, state_pool, cache_ckv, cache_kpe, w)
    return (y_prompt, y_sample, new_conv_prompt, new_conv_sample, new_h_prompt, new_h_sample,
            new_pool_prompt, new_pool_sample, new_ckv_prompt, new_ckv_sample, new_kpe_prompt, new_kpe_sample)
```

```python
import functools
import math

import jax
import jax.numpy as jnp
from jax import lax
from jax.experimental import pallas as pl
from jax.experimental.pallas import tpu as pltpu

F32 = jnp.float32
BF16 = jnp.bfloat16

D_MODEL = 2048
CHUNK = 64
RMS_EPS = 1e-6
LRU_WIDTH = 1024
LRU_HEADS = 8
LRU_HEAD_DIM = 128
CONV_WIDTH = 4
LRU_C = 8.0
POOL_WIDTH = 1024
POOL_WINDOWS = (2, 4, 8, 16)
POOL_GROUP_DIM = 256
POOL_MAX = 16
MIX_IN = 3072
MLA_HEADS = 16
Q_LORA = 512
KV_LORA = 256
NOPE_DIM = 128
ROPE_DIM = 64
V_DIM = 128
QK_DIM = NOPE_DIM + ROPE_DIM
ROPE_THETA = 10000.0
PEER_HEADS = 8
N_KEYS = 128
N_EXPERTS = N_KEYS * N_KEYS
PEER_TOPK = 16
HALF_KEY = 128

LANES = 128
QK_PAD = 256
NEG_BIG = -1e30
VMEM_LIMIT = 56 * 1024 * 1024


def _cparams(sem):
    return pltpu.CompilerParams(dimension_semantics=sem, vmem_limit_bytes=VMEM_LIMIT)


def _const_spec(shape):
    n = len(shape)
    return pl.BlockSpec(shape, lambda *_: (0,) * n)


def _seq_tiles(b, t):
    if t >= 512:
        return 1, 512
    sb = max(1, min(b, 512 // t))
    while b % sb:
        sb -= 1
    return sb, t


def _rmsnorm(x, g):
    return x * lax.rsqrt(jnp.mean(x * x, axis=-1, keepdims=True) + RMS_EPS) * g


def _norm_mod(x, g, sh, sc):
    return _rmsnorm(x, g) * (1.0 + sc) + sh


def _ada_kernel(c_ref, w_ref, b_ref, o_ref):
    c = c_ref[...]
    sc = (c * jax.nn.sigmoid(c)).astype(BF16)
    o_ref[0] = jnp.dot(sc, w_ref[0].astype(BF16), preferred_element_type=F32) + b_ref[0]


def _ada(c_all, ada_w, ada_b):
    depth, d, n = ada_w.shape
    rows = c_all.shape[0]
    tn = 1024
    return pl.pallas_call(
        _ada_kernel,
        grid=(depth, n // tn),
        in_specs=[pl.BlockSpec((rows, d), lambda l, j: (0, 0)),
                  pl.BlockSpec((1, d, tn), lambda l, j: (l, 0, j)),
                  pl.BlockSpec((1, 1, tn), lambda l, j: (l, 0, j))],
        out_specs=pl.BlockSpec((1, rows, tn), lambda l, j: (l, 0, j)),
        out_shape=jax.ShapeDtypeStruct((depth, rows, n), F32),
        compiler_params=_cparams(("arbitrary", "arbitrary")),
        name="ada_mod",
    )(c_all, ada_w, ada_b.reshape(depth, 1, n))


def _inproj_kernel(x_ref, g_ref, sh_ref, sc_ref, w_ref, o_ref):
    sb, tt, d = x_ref.shape
    hn = _norm_mod(x_ref[...], g_ref[...], sh_ref[...], sc_ref[...])
    z = jnp.dot(hn.reshape(sb * tt, d).astype(BF16), w_ref[...], preferred_element_type=F32)
    o_ref[...] = z.reshape(sb, tt, -1)


def _inproj(x, g, sh, sc, w):
    b, t, d = x.shape
    n = w.shape[1]
    sb, tt = _seq_tiles(b, t)
    tt = min(tt, 256)
    seq = pl.BlockSpec((sb, 1, d), lambda i, j: (i, 0, 0))
    return pl.pallas_call(
        _inproj_kernel,
        grid=(b // sb, t // tt),
        in_specs=[pl.BlockSpec((sb, tt, d), lambda i, j: (i, j, 0)),
                  _const_spec((1, d)), seq, seq, _const_spec((d, n))],
        out_specs=pl.BlockSpec((sb, tt, n), lambda i, j: (i, j, 0)),
        out_shape=jax.ShapeDtypeStruct((b, t, n), F32),
        compiler_params=_cparams(("arbitrary", "arbitrary")),
        name="l0_inproj",
    )(x, g, sh, sc, w)


def _shift_rows(x, d, fill):
    rolled = pltpu.roll(x, d, 0)
    rows = lax.broadcasted_iota(jnp.int32, x.shape, 0)
    return jnp.where(rows >= d, rolled, fill)


def _gelu_tanh(x):
    return 0.5 * x * (1.0 + jnp.tanh(math.sqrt(2.0 / math.pi) * (x + 0.044715 * (x * x * x))))


def _recmix_kernel(z_ref, conv0_ref, h0_ref, pool0_ref, cw_ref, cb_ref, wa_ref, ba_ref, wx_ref, bx_ref,
                   lam_ref, pw_ref, ps_ref, mix_ref, hlast_ref, ctail, ptail, hstate, ext, *, pos0):
    sb, tt, _ = z_ref.shape
    j = pl.program_id(1)

    @pl.when(j == 0)
    def _():
        ctail[...] = conv0_ref[...]
        ptail[...] = pool0_ref[...]
        hstate[...] = h0_ref[...]

    lam = lam_ref[...]
    neg = -lam
    softplus_neg_lam = jnp.maximum(neg, 0.0) + jnp.log1p(jnp.exp(-jnp.abs(neg)))
    pos = pos0 + j * tt + lax.broadcasted_iota(jnp.int32, (tt, 1), 0)

    def per_seq(s, carry):
        rec = z_ref[s, :, LRU_WIDTH:2 * LRU_WIDTH]
        ext[0:8, :] = ctail[s]
        ext[8:8 + tt, :] = rec
        xc = cb_ref[...] + sum(ext[5 + k:5 + k + tt, :] * cw_ref[k:k + 1, :] for k in range(CONV_WIDTH))
        ctail[s] = ext[tt:tt + 8, :]
        xcb = xc.astype(BF16)
        r_parts, i_parts = [], []
        for h in range(LRU_HEADS):
            sl = slice(h * LRU_HEAD_DIM, (h + 1) * LRU_HEAD_DIM)
            r_parts.append(jnp.dot(xcb[:, sl], wa_ref[h], preferred_element_type=F32))
            i_parts.append(jnp.dot(xcb[:, sl], wx_ref[h], preferred_element_type=F32))
        r = jax.nn.sigmoid(jnp.concatenate(r_parts, axis=-1) + ba_ref[...])
        i = jax.nn.sigmoid(jnp.concatenate(i_parts, axis=-1) + bx_ref[...])
        log_a = -LRU_C * r * softplus_neg_lam
        a = jnp.exp(log_a)
        th = jnp.tanh(log_a)
        mult = jnp.sqrt(-2.0 * th / (1.0 - th))
        mult = jnp.where(pos == 0, 1.0, mult)
        u = mult * (i * xc)
        acc_a, acc_b = a, u
        d = 1
        while d < tt:
            a_sh = _shift_rows(acc_a, d, 1.0)
            b_sh = _shift_rows(acc_b, d, 0.0)
            acc_b = acc_a * b_sh + acc_b
            acc_a = acc_a * a_sh
            d *= 2
        hseq = acc_a * hstate[s] + acc_b
        hstate[s] = hseq[tt - 1:tt, :]
        gate = z_ref[s, :, 0:LRU_WIDTH]
        mix_ref[s, :, 0:LRU_WIDTH] = (_gelu_tanh(gate) * hseq).astype(mix_ref.dtype)
        pin = z_ref[s, :, 2 * LRU_WIDTH:]
        ext[0:16, :] = ptail[s]
        ext[16:16 + tt, :] = pin
        ptail[s] = ext[tt:tt + 16, :]
        for g, w in enumerate(POOL_WINDOWS):
            sl = slice(g * POOL_GROUP_DIM, (g + 1) * POOL_GROUP_DIM)
            win = sum(ext[16 - k:16 - k + tt, sl] for k in range(w))
            cnt = jnp.minimum(pos + 1, w).astype(F32)
            pooled = win / cnt - pin[:, sl]
            bo = jnp.dot(pooled.astype(BF16), pw_ref[g], preferred_element_type=F32) * ps_ref[:, sl]
            mix_ref[s, :, LRU_WIDTH + g * POOL_GROUP_DIM:LRU_WIDTH + (g + 1) * POOL_GROUP_DIM] = bo.astype(mix_ref.dtype)
        return carry

    lax.fori_loop(0, sb, per_seq, 0)

    @pl.when(j == pl.num_programs(1) - 1)
    def _():
        hlast_ref[...] = hstate[...]


def _recmix(z, conv0, h0, pool0, cw, cb, wa, ba, wx, bx, lam, pw, ps, pos0):
    b, t, _ = z.shape
    sb, tt = _seq_tiles(b, t)
    tt = min(tt, 256)
    c = LRU_WIDTH
    row = _const_spec((1, c))
    return pl.pallas_call(
        functools.partial(_recmix_kernel, pos0=pos0),
        grid=(b // sb, t // tt),
        in_specs=[pl.BlockSpec((sb, tt, MIX_IN), lambda i, j: (i, j, 0)),
                  pl.BlockSpec((sb, 8, c), lambda i, j: (i, 0, 0)),
                  pl.BlockSpec((sb, 1, c), lambda i, j: (i, 0, 0)),
                  pl.BlockSpec((sb, 16, c), lambda i, j: (i, 0, 0)),
                  _const_spec((CONV_WIDTH, c)), row,
                  _const_spec((LRU_HEADS, LRU_HEAD_DIM, LRU_HEAD_DIM)), row,
                  _const_spec((LRU_HEADS, LRU_HEAD_DIM, LRU_HEAD_DIM)), row, row,
                  _const_spec((len(POOL_WINDOWS), POOL_GROUP_DIM, POOL_GROUP_DIM)), row],
        out_specs=[pl.BlockSpec((sb, tt, 2 * c), lambda i, j: (i, j, 0)),
                   pl.BlockSpec((sb, 1, c), lambda i, j: (i, 0, 0))],
        out_shape=[jax.ShapeDtypeStruct((b, t, 2 * c), BF16),
                   jax.ShapeDtypeStruct((b, 1, c), F32)],
        scratch_shapes=[pltpu.VMEM((sb, 8, c), F32), pltpu.VMEM((sb, 16, c), F32),
                        pltpu.VMEM((sb, 1, c), F32), pltpu.VMEM((tt + 16, c), F32)],
        compiler_params=_cparams(("arbitrary", "arbitrary")),
        name="l0_recmix",
    )(z, conv0, h0, pool0, cw, cb, wa, ba, wx, bx, lam, pw, ps)


def _proj_res_kernel(a_ref, w_ref, x_ref, g1_ref, n2_ref, sh_ref, sc_ref, xo_ref, ho_ref):
    sb, tt, k = a_ref.shape
    mix = jnp.dot(a_ref[...].reshape(sb * tt, k), w_ref[...], preferred_element_type=F32)
    xn = x_ref[...] + g1_ref[...] * mix.reshape(sb, tt, -1)
    xo_ref[...] = xn
    ho_ref[...] = _norm_mod(xn, n2_ref[...], sh_ref[...], sc_ref[...]).astype(ho_ref.dtype)


def _proj_res(a, w, x, g1, n2, sh2, sc2):
    b, t, d = x.shape
    k = a.shape[-1]
    sb, tt = _seq_tiles(b, t)
    seq = pl.BlockSpec((sb, 1, d), lambda i, j: (i, 0, 0))
    tile = pl.BlockSpec((sb, tt, d), lambda i, j: (i, j, 0))
    return pl.pallas_call(
        _proj_res_kernel,
        grid=(b // sb, t // tt),
        in_specs=[pl.BlockSpec((sb, tt, k), lambda i, j: (i, j, 0)), _const_spec((k, d)), tile,
                  seq, _const_spec((1, d)), seq, seq],
        out_specs=[tile, tile],
        out_shape=[jax.ShapeDtypeStruct((b, t, d), F32), jax.ShapeDtypeStruct((b, t, d), BF16)],
        compiler_params=_cparams(("arbitrary", "arbitrary")),
        name="proj_res_norm",
    )(a, w, x, g1, n2, sh2, sc2)


def _top16_rows(vals):
    tm = vals.shape[1]
    rows16 = lax.broadcasted_iota(jnp.int32, (PEER_TOPK, tm), 0)
    top = jnp.zeros((PEER_TOPK, tm), F32)
    for k in range(PEER_TOPK):
        mx = jnp.max(vals, axis=0, keepdims=True)
        top = jnp.where(rows16 == k, mx, top)
        vals = jnp.where(vals == mx, -jnp.inf, vals)
    return top


def _peer_route_kernel(x_ref, wqt_ref, keys_ref, e1_ref, e2_ref, th_ref, qt_ref):
    qt_ref[...] = lax.dot_general(wqt_ref[...], x_ref[...], (((1,), (1,)), ((), ())),
                                  preferred_element_type=F32).astype(BF16)

    def per_head(h, carry):
        r1 = pl.multiple_of(h * 2 * HALF_KEY, 2 * HALF_KEY)
        r2 = pl.multiple_of(h * 2 * HALF_KEY + HALF_KEY, HALF_KEY)
        s1 = jnp.dot(keys_ref[2 * h], qt_ref[pl.ds(r1, HALF_KEY), :], preferred_element_type=F32)
        s2 = jnp.dot(keys_ref[2 * h + 1], qt_ref[pl.ds(r2, HALF_KEY), :], preferred_element_type=F32)
        t1 = _top16_rows(s1)
        t2 = _top16_rows(s2)
        cand = jnp.concatenate([t1[a:a + 1, :] + t2 for a in range(PEER_TOPK)], axis=0)
        vals = cand
        tau = None
        for _ in range(PEER_TOPK):
            tau = jnp.max(vals, axis=0, keepdims=True)
            vals = jnp.where(vals == tau, -jnp.inf, vals)
        sel = cand >= tau
        m1 = t1[0:1, :]
        m2 = t2[0:1, :]
        z = jnp.sum(jnp.where(sel, jnp.exp(cand - (m1 + m2)), 0.0), axis=0, keepdims=True)
        rz = 1.0 / z
        e1_ref[h] = jnp.exp(s1 - m1) * rz
        e2_ref[h] = jnp.exp(s2 - m2)
        pt1 = jnp.exp(t1 - m1) * rz
        pt2 = jnp.exp(t2 - m2)
        prod = jnp.concatenate([pt1[a:a + 1, :] * pt2 for a in range(PEER_TOPK)], axis=0)
        th_ref[h] = jnp.min(jnp.where(sel, prod, jnp.inf), axis=0, keepdims=True)
        return carry

    lax.fori_loop(0, PEER_HEADS, per_head, 0)


def _peer_route(hn, wqt, keys):
    m, d = hn.shape
    tm = 256
    return pl.pallas_call(
        _peer_route_kernel,
        grid=(m // tm,),
        in_specs=[pl.BlockSpec((tm, d), lambda i: (i, 0)), _const_spec((d, d)),
                  _const_spec((2 * PEER_HEADS, N_KEYS, HALF_KEY))],
        out_specs=[pl.BlockSpec((PEER_HEADS, N_KEYS, tm), lambda i: (0, 0, i)),
                   pl.BlockSpec((PEER_HEADS, N_KEYS, tm), lambda i: (0, 0, i)),
                   pl.BlockSpec((PEER_HEADS, 1, tm), lambda i: (0, 0, i))],
        out_shape=[jax.ShapeDtypeStruct((PEER_HEADS, N_KEYS, m), F32),
                   jax.ShapeDtypeStruct((PEER_HEADS, N_KEYS, m), F32),
                   jax.ShapeDtypeStruct((PEER_HEADS, 1, m), F32)],
        scratch_shapes=[pltpu.VMEM((d, tm), BF16)],
        compiler_params=_cparams(("arbitrary",)),
        name="peer_route",
    )(hn, wqt, keys)


def _gelu_erf(x):
    return 0.5 * x * (1.0 + lax.erf(x * math.sqrt(0.5)))


def _peer_expert_kernel(x_ref, u_ref, vt_ref, e1_ref, e2_ref, th_ref, o_ref, acc_ref, s_ref, a_ref):
    e = pl.program_id(1)
    eb, tm = s_ref.shape

    @pl.when(e == 0)
    def _():
        acc_ref[...] = jnp.zeros_like(acc_ref)

    s_ref[...] = lax.dot_general(u_ref[...], x_ref[...], (((1,), (1,)), ((), ())), preferred_element_type=F32)

    def lane_tile(lt, carry):
        col = pl.multiple_of(lt * LANES, LANES)
        for ii in range(eb // N_KEYS):
            g = jnp.zeros((N_KEYS, LANES), F32)
            for h in range(PEER_HEADS):
                p = e2_ref[h, :, pl.ds(col, LANES)] * e1_ref[h, ii:ii + 1, pl.ds(col, LANES)]
                g = g + jnp.where(p >= th_ref[h, :, pl.ds(col, LANES)], p, 0.0)
            s = s_ref[ii * N_KEYS:(ii + 1) * N_KEYS, pl.ds(col, LANES)]
            a_ref[ii * N_KEYS:(ii + 1) * N_KEYS, pl.ds(col, LANES)] = (_gelu_erf(s) * g).astype(a_ref.dtype)
        return carry

    lax.fori_loop(0, tm // LANES, lane_tile, 0)
    acc_ref[...] += jnp.dot(vt_ref[...], a_ref[...], preferred_element_type=F32)

    @pl.when(e == pl.num_programs(1) - 1)
    def _():
        o_ref[...] = acc_ref[...].T


def _peer_experts(hn, u, vt, e1, e2, th):
    m, d = hn.shape
    tm = 512
    eb = 1024
    return pl.pallas_call(
        _peer_expert_kernel,
        grid=(m // tm, N_EXPERTS // eb),
        in_specs=[pl.BlockSpec((tm, d), lambda i, e: (i, 0)),
                  pl.BlockSpec((eb, d), lambda i, e: (e, 0)),
                  pl.BlockSpec((d, eb), lambda i, e: (0, e)),
                  pl.BlockSpec((PEER_HEADS, eb // N_KEYS, tm), lambda i, e: (0, e, i)),
                  pl.BlockSpec((PEER_HEADS, N_KEYS, tm), lambda i, e: (0, 0, i)),
                  pl.BlockSpec((PEER_HEADS, 1, tm), lambda i, e: (0, 0, i))],
        out_specs=pl.BlockSpec((tm, d), lambda i, e: (i, 0)),
        out_shape=jax.ShapeDtypeStruct((m, d), F32),
        scratch_shapes=[pltpu.VMEM((d, tm), F32), pltpu.VMEM((eb, tm), F32), pltpu.VMEM((eb, tm), BF16)],
        compiler_params=_cparams(("arbitrary", "arbitrary")),
        name="peer_experts",
    )(hn, u, vt, e1, e2, th)


def _res_kernel(x_ref, p_ref, g_ref, o_ref):
    o_ref[...] = x_ref[...] + g_ref[...] * p_ref[...]


def _res_norm_kernel(x_ref, p_ref, g_ref, fg_ref, o_ref):
    o_ref[...] = _rmsnorm(x_ref[...] + g_ref[...] * p_ref[...], fg_ref[...])


def _residual(x, p, g2, final_g=None):
    b, t, d = x.shape
    sb, tt = _seq_tiles(b, t)
    tile = pl.BlockSpec((sb, tt, d), lambda i, j: (i, j, 0))
    seq = pl.BlockSpec((sb, 1, d), lambda i, j: (i, 0, 0))
    in_specs = [tile, tile, seq]
    args = [x, p, g2]
    body = _res_kernel
    if final_g is not None:
        in_specs.append(_const_spec((1, d)))
        args.append(final_g)
        body = _res_norm_kernel
    return pl.pallas_call(
        body, grid=(b // sb, t // tt), in_specs=in_specs, out_specs=tile,
        out_shape=jax.ShapeDtypeStruct((b, t, d), F32),
        compiler_params=_cparams(("arbitrary", "arbitrary")),
        name="peer_residual",
    )(*args)


def _qkv_kernel(x_ref, g_ref, sh_ref, sc_ref, wdq_ref, gq_ref, wq1_ref, wq2_ref, wdkv_ref, gkv_ref,
                cos_ref, sin_ref, q_ref, ckv_ref, kpe_ref):
    sb, tt, d = x_ref.shape
    hn = _norm_mod(x_ref[...], g_ref[...], sh_ref[...], sc_ref[...]).reshape(sb * tt, d).astype(BF16)
    cq = _rmsnorm(jnp.dot(hn, wdq_ref[...], preferred_element_type=F32), gq_ref[...]).astype(BF16)
    q1 = jnp.dot(cq, wq1_ref[...], preferred_element_type=F32)
    q2 = jnp.dot(cq, wq2_ref[...], preferred_element_type=F32)
    cos = cos_ref[...][None]
    sin = sin_ref[...][None]
    for h in range(MLA_HEADS):
        q_ref[:, :, h * QK_PAD:h * QK_PAD + NOPE_DIM] = (
            q1[:, h * QK_PAD:h * QK_PAD + NOPE_DIM].reshape(sb, tt, NOPE_DIM).astype(q_ref.dtype))
        pe = (q1[:, h * QK_PAD + NOPE_DIM:(h + 1) * QK_PAD].reshape(sb, tt, LANES) * cos
              + q2[:, h * LANES:(h + 1) * LANES].reshape(sb, tt, LANES) * sin)
        q_ref[:, :, h * QK_PAD + NOPE_DIM:(h + 1) * QK_PAD] = pe.astype(q_ref.dtype)
    kv = jnp.dot(hn, wdkv_ref[...], preferred_element_type=F32)
    ckv_ref[...] = _rmsnorm(kv[:, :KV_LORA], gkv_ref[...]).reshape(sb, tt, KV_LORA)
    kpe = (kv[:, KV_LORA:KV_LORA + LANES].reshape(sb, tt, LANES) * cos
           + kv[:, KV_LORA + LANES:].reshape(sb, tt, LANES) * sin)
    kpe_ref[...] = kpe[:, :, :ROPE_DIM]


def _qkv(x, g, sh, sc, wdq, gq, wq1, wq2, wdkv, gkv, cos, sin):
    b, t, d = x.shape
    sb, tt = _seq_tiles(b, t)
    tt = min(tt, 256)
    seq = pl.BlockSpec((sb, 1, d), lambda i, j: (i, 0, 0))
    rope = pl.BlockSpec((tt, LANES), lambda i, j: (j, 0))
    return pl.pallas_call(
        _qkv_kernel,
        grid=(b // sb, t // tt),
        in_specs=[pl.BlockSpec((sb, tt, d), lambda i, j: (i, j, 0)), _const_spec((1, d)), seq, seq,
                  _const_spec(wdq.shape), _const_spec((1, Q_LORA)), _const_spec(wq1.shape), _const_spec(wq2.shape),
                  _const_spec(wdkv.shape), _const_spec((1, KV_LORA)), rope, rope],
        out_specs=[pl.BlockSpec((sb, tt, MLA_HEADS * QK_PAD), lambda i, j: (i, j, 0)),
                   pl.BlockSpec((sb, tt, KV_LORA), lambda i, j: (i, j, 0)),
                   pl.BlockSpec((sb, tt, ROPE_DIM), lambda i, j: (i, j, 0))],
        out_shape=[jax.ShapeDtypeStruct((b, t, MLA_HEADS * QK_PAD), BF16),
                   jax.ShapeDtypeStruct((b, t, KV_LORA), F32),
                   jax.ShapeDtypeStruct((b, t, ROPE_DIM), F32)],
        compiler_params=_cparams(("arbitrary", "arbitrary")),
        name="l1_qkv",
    )(x, g, sh, sc, wdq, gq, wq1, wq2, wdkv, gkv, cos, sin)


def _kvup_kernel(ckv_ref, kpe_ref, wk_ref, wv_ref, k_ref, v_ref):
    c = ckv_ref[0].astype(BF16)
    kn = jnp.dot(c, wk_ref[...], preferred_element_type=F32)
    v_ref[0] = jnp.dot(c, wv_ref[...], preferred_element_type=F32).astype(v_ref.dtype)
    kpe = kpe_ref[0].astype(k_ref.dtype)
    for h in range(MLA_HEADS):
        k_ref[0, :, h * QK_PAD:h * QK_PAD + NOPE_DIM] = kn[:, h * NOPE_DIM:(h + 1) * NOPE_DIM].astype(k_ref.dtype)
        k_ref[0, :, h * QK_PAD + NOPE_DIM:(h + 1) * QK_PAD] = kpe


def _kvup(ckv, kpe128, wk, wv):
    b, t, _ = ckv.shape
    tt = max(c for c in range(16, 513, 16) if t % c == 0)
    return pl.pallas_call(
        _kvup_kernel,
        grid=(b, t // tt),
        in_specs=[pl.BlockSpec((1, tt, KV_LORA), lambda i, j: (i, j, 0)),
                  pl.BlockSpec((1, tt, LANES), lambda i, j: (i, j, 0)),
                  _const_spec(wk.shape), _const_spec(wv.shape)],
        out_specs=[pl.BlockSpec((1, tt, MLA_HEADS * QK_PAD), lambda i, j: (i, j, 0)),
                   pl.BlockSpec((1, tt, MLA_HEADS * V_DIM), lambda i, j: (i, j, 0))],
        out_shape=[jax.ShapeDtypeStruct((b, t, MLA_HEADS * QK_PAD), BF16),
                   jax.ShapeDtypeStruct((b, t, MLA_HEADS * V_DIM), BF16)],
        compiler_params=_cparams(("arbitrary", "arbitrary")),
        name="l1_kvup",
    )(ckv, kpe128, wk, wv)


def _attn_kernel(q_ref, k_ref, v_ref, o_ref, *, tk, q_off):
    tq = q_ref.shape[1]
    qi = pl.program_id(2)
    q = q_ref[0]
    q0 = q_off + qi * tq
    n_full = (((q0 // CHUNK) + 1) * CHUNK) // tk
    k_end = ((q0 + tq - 1) // CHUNK + 1) * CHUNK
    n_tot = (k_end + tk - 1) // tk
    scale = QK_DIM ** -0.5

    def step(ki, carry, masked):
        m, l, acc = carry
        start = pl.multiple_of(ki * tk, tk)
        s = lax.dot_general(q, k_ref[0, pl.ds(start, tk), :], (((1,), (1,)), ((), ())),
                            preferred_element_type=F32) * scale
        if masked:
            qc = (q0 + lax.broadcasted_iota(jnp.int32, (tq, tk), 0)) // CHUNK
            kc = (start + lax.broadcasted_iota(jnp.int32, (tq, tk), 1)) // CHUNK
            s = jnp.where(kc <= qc, s, NEG_BIG)
        m_new = jnp.maximum(m, jnp.max(s, axis=-1, keepdims=True))
        p = jnp.exp(s - m_new)
        alpha = jnp.exp(m - m_new)
        l = alpha * l + jnp.sum(p, axis=-1, keepdims=True)
        acc = alpha * acc + jnp.dot(p.astype(BF16), v_ref[0, pl.ds(start, tk), :], preferred_element_type=F32)
        return m_new, l, acc

    init = (jnp.full((tq, 1), NEG_BIG, F32), jnp.zeros((tq, 1), F32), jnp.zeros((tq, V_DIM), F32))
    carry = lax.fori_loop(0, n_full, functools.partial(step, masked=False), init)
    _, l, acc = lax.fori_loop(n_full, n_tot, functools.partial(step, masked=True), carry)
    o_ref[0] = (acc / l).astype(o_ref.dtype)


def _attention(q, k, v):
    b, tq_all, _ = q.shape
    tk_all = k.shape[1]
    tq = min(tq_all, 512)
    tk = 512 if tk_all % 512 == 0 else tk_all
    return pl.pallas_call(
        functools.partial(_attn_kernel, tk=tk, q_off=tk_all - tq_all),
        grid=(b, MLA_HEADS, tq_all // tq),
        in_specs=[pl.BlockSpec((1, tq, QK_PAD), lambda i, h, j: (i, j, h)),
                  pl.BlockSpec((1, tk_all, QK_PAD), lambda i, h, j: (i, 0, h)),
                  pl.BlockSpec((1, tk_all, V_DIM), lambda i, h, j: (i, 0, h))],
        out_specs=pl.BlockSpec((1, tq, V_DIM), lambda i, h, j: (i, j, h)),
        out_shape=jax.ShapeDtypeStruct((b, tq_all, MLA_HEADS * V_DIM), BF16),
        compiler_params=_cparams(("arbitrary", "arbitrary", "arbitrary")),
        name="l1_attention",
    )(q, k, v)


def _prep_weights(w):
    p = {}
    p["rec_w_in"] = w["rec_w_in"][0].astype(BF16)
    p["rec_w_a"] = w["rec_w_a"][0].astype(BF16)
    p["rec_w_x"] = w["rec_w_x"][0].astype(BF16)
    p["pool_w"] = w["pool_w"][0].astype(BF16)
    p["rec_w_out"] = w["rec_w_out"][0].astype(BF16)
    p["att_w_dq"] = w["att_w_dq"][0].astype(BF16)
    wuq = w["att_w_uq"][0].reshape(Q_LORA, MLA_HEADS, QK_DIM)
    half = ROPE_DIM // 2
    zpad = jnp.zeros((Q_LORA, MLA_HEADS, QK_PAD - QK_DIM), F32)
    p["wq1"] = jnp.concatenate([wuq, zpad], axis=-1).reshape(Q_LORA, MLA_HEADS * QK_PAD).astype(BF16)
    swapped = jnp.concatenate([wuq[..., NOPE_DIM + half:], wuq[..., NOPE_DIM:NOPE_DIM + half],
                               jnp.zeros((Q_LORA, MLA_HEADS, LANES - ROPE_DIM), F32)], axis=-1)
    p["wq2"] = swapped.reshape(Q_LORA, MLA_HEADS * LANES).astype(BF16)
    wdkv = w["att_w_dkv"][0]
    zl = jnp.zeros((D_MODEL, LANES - ROPE_DIM), F32)
    p["wdkv"] = jnp.concatenate([wdkv, zl, wdkv[:, KV_LORA + half:], wdkv[:, KV_LORA:KV_LORA + half], zl],
                                axis=-1).astype(BF16)
    wukv = w["att_w_ukv"][0].reshape(KV_LORA, MLA_HEADS, NOPE_DIM + V_DIM)
    p["wk"] = wukv[..., :NOPE_DIM].reshape(KV_LORA, MLA_HEADS * NOPE_DIM).astype(BF16)
    p["wv"] = wukv[..., NOPE_DIM:].reshape(KV_LORA, MLA_HEADS * V_DIM).astype(BF16)
    p["att_w_out"] = w["att_w_out"][0].astype(BF16)
    p["peer_wqt"] = [w["peer_w_q"][l].T.astype(BF16) for l in range(2)]
    p["peer_keys"] = [w["peer_keys"][l].reshape(2 * PEER_HEADS, N_KEYS, HALF_KEY).astype(BF16) for l in range(2)]
    p["peer_u"] = [w["peer_u"][l].astype(BF16) for l in range(2)]
    p["peer_vt"] = [w["peer_v"][l].astype(BF16).T for l in range(2)]
    return p


def _rope_tables(pos):
    half = ROPE_DIM // 2
    inv = ROPE_THETA ** (-(jnp.arange(half, dtype=F32) / half))
    ang = pos.astype(F32)[:, None] * inv[None, :]
    cos = jnp.cos(ang)
    sin = jnp.sin(ang)
    z = jnp.zeros((pos.shape[0], LANES - ROPE_DIM), F32)
    return jnp.concatenate([cos, cos, z], axis=-1), jnp.concatenate([-sin, sin, z], axis=-1)


def _peer(hn, p, l):
    b, t, d = hn.shape
    flat = hn.reshape(b * t, d)
    e1, e2, th = _peer_route(flat, p["peer_wqt"][l], p["peer_keys"][l])
    return _peer_experts(flat, p["peer_u"][l], p["peer_vt"][l], e1, e2, th).reshape(b, t, d)


def _trunk(x, mods, pos0, conv_prev, h_prev, pool_prev, ckv_prev, kpe_prev, w, p):
    b, t, d = x.shape
    row = lambda a: a.reshape(1, -1)
    sh1, sc1, g1, sh2, sc2, g2 = mods[0]
    z = _inproj(x, row(w["norm1_g"][0]), sh1, sc1, p["rec_w_in"])
    conv0 = jnp.pad(conv_prev, ((0, 0), (8 - (CONV_WIDTH - 1), 0), (0, 0)))
    pool0 = jnp.pad(pool_prev, ((0, 0), (1, 0), (0, 0)))
    mix, h_last = _recmix(z, conv0, h_prev[:, None, :], pool0, w["rec_conv_w"][0], row(w["rec_conv_b"][0]),
                          p["rec_w_a"], row(w["rec_b_a"][0]), p["rec_w_x"], row(w["rec_b_x"][0]),
                          row(w["rec_lambda"][0]), p["pool_w"], row(w["pool_scale"][0]), pos0)
    new_conv = z[:, t - (CONV_WIDTH - 1):, LRU_WIDTH:2 * LRU_WIDTH]
    new_pool = z[:, t - (POOL_MAX - 1):, 2 * LRU_WIDTH:]
    x, hn = _proj_res(mix, p["rec_w_out"], x, g1, row(w["norm2_g"][0]), sh2, sc2)
    x = _residual(x, _peer(hn, p, 0), g2)
    sh1, sc1, g1, sh2, sc2, g2 = mods[1]
    pos = pos0 + jnp.arange(t, dtype=jnp.int32)
    cos, sin = _rope_tables(pos)
    q, ckv, kpe = _qkv(x, row(w["norm1_g"][1]), sh1, sc1, p["att_w_dq"], row(w["att_g_q"][0]), p["wq1"], p["wq2"],
                       p["wdkv"], row(w["att_g_kv"][0]), cos, sin)
    ckv_all = jnp.concatenate([ckv_prev, ckv], axis=1)
    kpe_all = jnp.concatenate([kpe_prev, kpe], axis=1)
    kpe128 = jnp.pad(kpe_all, ((0, 0), (0, 0), (0, LANES - ROPE_DIM)))
    k, v = _kvup(ckv_all, kpe128, p["wk"], p["wv"])
    o = _attention(q, k, v)
    x, hn = _proj_res(o, p["att_w_out"], x, g1, row(w["norm2_g"][1]), sh2, sc2)
    y = _residual(x, _peer(hn, p, 1), g2, row(w["final_g"]))
    return y, new_conv[None], h_last.reshape(1, b, LRU_WIDTH), new_pool[None], ckv[None], kpe[None]


def kernel(x_prompt, x_sample, c_prompt, c_sample, state_conv, state_lru_h, state_pool, cache_ckv, cache_kpe,
           ada_w, ada_b, norm1_g, norm2_g, rec_w_in, rec_conv_w, rec_conv_b, rec_w_a, rec_b_a, rec_w_x, rec_b_x,
           rec_lambda, pool_w, pool_scale, rec_w_out, att_w_dq, att_g_q, att_w_uq, att_w_dkv, att_g_kv, att_w_ukv,
           att_w_out, peer_w_q, peer_keys, peer_u, peer_v, final_g):
    w = dict(ada_w=ada_w, ada_b=ada_b, norm1_g=norm1_g, norm2_g=norm2_g, rec_w_in=rec_w_in, rec_conv_w=rec_conv_w,
             rec_conv_b=rec_conv_b, rec_w_a=rec_w_a, rec_b_a=rec_b_a, rec_w_x=rec_w_x, rec_b_x=rec_b_x,
             rec_lambda=rec_lambda, pool_w=pool_w, pool_scale=pool_scale, rec_w_out=rec_w_out, att_w_dq=att_w_dq,
             att_g_q=att_g_q, att_w_uq=att_w_uq, att_w_dkv=att_w_dkv, att_g_kv=att_g_kv, att_w_ukv=att_w_ukv,
             att_w_out=att_w_out, peer_w_q=peer_w_q, peer_keys=peer_keys, peer_u=peer_u, peer_v=peer_v,
             final_g=final_g)
    p = _prep_weights(w)
    bp, tp, d = x_prompt.shape
    bs, ts, _ = x_sample.shape
    past = cache_ckv.shape[2]
    depth = ada_w.shape[0]
    rows = bp + bs
    rows_pad = -(-rows // 8) * 8
    c_all = jnp.pad(jnp.concatenate([c_prompt, c_sample], axis=0), ((0, rows_pad - rows), (0, 0)))
    mod = _ada(c_all, ada_w, ada_b)

    def mods_of(lo, n):
        return [[mod[l, lo:lo + n, k * d:(k + 1) * d][:, None, :] for k in range(6)] for l in range(depth)]

    zeros = lambda *s: jnp.zeros(s, F32)
    out_p = _trunk(x_prompt, mods_of(0, bp), 0, zeros(bp, CONV_WIDTH - 1, LRU_WIDTH), zeros(bp, LRU_WIDTH),
                   zeros(bp, POOL_MAX - 1, POOL_WIDTH), zeros(bp, 0, KV_LORA), zeros(bp, 0, ROPE_DIM), w, p)
    out_s = _trunk(x_sample, mods_of(bp, bs), past, state_conv[0], state_lru_h[0], state_pool[0],
                   cache_ckv[0], cache_kpe[0], w, p)
    return (out_p[0], out_s[0], out_p[1], out_s[1], out_p[2], out_s[2],
            out_p[3], out_s[3], out_p[4], out_s[4], out_p[5], out_s[5])
```

```python
import functools
import math

import jax
import jax.numpy as jnp
from jax import lax
from jax.experimental import pallas as pl
from jax.experimental.pallas import tpu as pltpu

F32 = jnp.float32
BF16 = jnp.bfloat16

D_MODEL = 2048
CHUNK = 64
RMS_EPS = 1e-6
LRU_WIDTH = 1024
LRU_HEADS = 8
LRU_HEAD_DIM = 128
CONV_WIDTH = 4
LRU_C = 8.0
POOL_WIDTH = 1024
POOL_WINDOWS = (2, 4, 8, 16)
POOL_GROUP_DIM = 256
POOL_MAX = 16
MIX_IN = 3072
MLA_HEADS = 16
Q_LORA = 512
KV_LORA = 256
NOPE_DIM = 128
ROPE_DIM = 64
V_DIM = 128
QK_DIM = NOPE_DIM + ROPE_DIM
ROPE_THETA = 10000.0
PEER_HEADS = 8
N_KEYS = 128
N_EXPERTS = N_KEYS * N_KEYS
PEER_TOPK = 16
HALF_KEY = 128

LANES = 128
GATE_ROWS = 64
EXPERT_SUB = 256
QK_PAD = 256
ATTN_TILE = 512
SOFTMAX_ROWS = 64
Q_SCALE = QK_DIM ** -0.5 * math.log2(math.e)
NEG_BIG = -1e30
VMEM_LIMIT = 56 * 1024 * 1024


def _cparams(sem):
    return pltpu.CompilerParams(dimension_semantics=sem, vmem_limit_bytes=VMEM_LIMIT)


def _const_spec(shape):
    n = len(shape)
    return pl.BlockSpec(shape, lambda *_: (0,) * n)


def _seq_tiles(b, t):
    if t >= 512:
        return 1, 512
    sb = max(1, min(b, 512 // t))
    while b % sb:
        sb -= 1
    return sb, t


def _rmsnorm(x, g):
    return x * lax.rsqrt(jnp.mean(x * x, axis=-1, keepdims=True) + RMS_EPS) * g


def _norm_mod(x, g, sh, sc):
    return _rmsnorm(x, g) * (1.0 + sc) + sh


def _ada_kernel(c_ref, w_ref, b_ref, o_ref):
    c = c_ref[...]
    sc = (c * jax.nn.sigmoid(c)).astype(BF16)
    o_ref[0] = jnp.dot(sc, w_ref[0].astype(BF16), preferred_element_type=F32) + b_ref[0]


def _ada(c_all, ada_w, ada_b):
    depth, d, n = ada_w.shape
    rows = c_all.shape[0]
    tn = 1024
    return pl.pallas_call(
        _ada_kernel,
        grid=(depth, n // tn),
        in_specs=[pl.BlockSpec((rows, d), lambda l, j: (0, 0)),
                  pl.BlockSpec((1, d, tn), lambda l, j: (l, 0, j)),
                  pl.BlockSpec((1, 1, tn), lambda l, j: (l, 0, j))],
        out_specs=pl.BlockSpec((1, rows, tn), lambda l, j: (l, 0, j)),
        out_shape=jax.ShapeDtypeStruct((depth, rows, n), F32),
        compiler_params=_cparams(("arbitrary", "arbitrary")),
        name="ada_mod",
    )(c_all, ada_w, ada_b.reshape(depth, 1, n))


def _inproj_kernel(x_ref, g_ref, sh_ref, sc_ref, w_ref, o_ref):
    sb, tt, d = x_ref.shape
    hn = _norm_mod(x_ref[...], g_ref[...], sh_ref[...], sc_ref[...])
    z = jnp.dot(hn.reshape(sb * tt, d).astype(BF16), w_ref[...], preferred_element_type=F32)
    o_ref[...] = z.reshape(sb, tt, -1)


def _inproj(x, g, sh, sc, w):
    b, t, d = x.shape
    n = w.shape[1]
    sb, tt = _seq_tiles(b, t)
    tt = min(tt, 256)
    seq = pl.BlockSpec((sb, 1, d), lambda i, j: (i, 0, 0))
    return pl.pallas_call(
        _inproj_kernel,
        grid=(b // sb, t // tt),
        in_specs=[pl.BlockSpec((sb, tt, d), lambda i, j: (i, j, 0)),
                  _const_spec((1, d)), seq, seq, _const_spec((d, n))],
        out_specs=pl.BlockSpec((sb, tt, n), lambda i, j: (i, j, 0)),
        out_shape=jax.ShapeDtypeStruct((b, t, n), F32),
        compiler_params=_cparams(("arbitrary", "arbitrary")),
        name="l0_inproj",
    )(x, g, sh, sc, w)


def _shift_rows(x, d, fill):
    rolled = pltpu.roll(x, d, 0)
    rows = lax.broadcasted_iota(jnp.int32, x.shape, 0)
    return jnp.where(rows >= d, rolled, fill)


def _gelu_tanh(x):
    return 0.5 * x * (1.0 + jnp.tanh(math.sqrt(2.0 / math.pi) * (x + 0.044715 * (x * x * x))))


def _recmix_kernel(z_ref, conv0_ref, h0_ref, pool0_ref, cw_ref, cb_ref, wa_ref, ba_ref, wx_ref, bx_ref,
                   lam_ref, pw_ref, ps_ref, mix_ref, hlast_ref, ctail, ptail, hstate, ext, *, pos0):
    sb, tt, _ = z_ref.shape
    j = pl.program_id(1)

    @pl.when(j == 0)
    def _():
        ctail[...] = conv0_ref[...]
        ptail[...] = pool0_ref[...]
        hstate[...] = h0_ref[...]

    lam = lam_ref[...]
    neg = -lam
    softplus_neg_lam = jnp.maximum(neg, 0.0) + jnp.log1p(jnp.exp(-jnp.abs(neg)))
    pos = pos0 + j * tt + lax.broadcasted_iota(jnp.int32, (tt, 1), 0)

    def per_seq(s, carry):
        rec = z_ref[s, :, LRU_WIDTH:2 * LRU_WIDTH]
        ext[0:8, :] = ctail[s]
        ext[8:8 + tt, :] = rec
        xc = cb_ref[...] + sum(ext[5 + k:5 + k + tt, :] * cw_ref[k:k + 1, :] for k in range(CONV_WIDTH))
        ctail[s] = ext[tt:tt + 8, :]
        xcb = xc.astype(BF16)
        r_parts, i_parts = [], []
        for h in range(LRU_HEADS):
            sl = slice(h * LRU_HEAD_DIM, (h + 1) * LRU_HEAD_DIM)
            r_parts.append(jnp.dot(xcb[:, sl], wa_ref[h], preferred_element_type=F32))
            i_parts.append(jnp.dot(xcb[:, sl], wx_ref[h], preferred_element_type=F32))
        r = jax.nn.sigmoid(jnp.concatenate(r_parts, axis=-1) + ba_ref[...])
        i = jax.nn.sigmoid(jnp.concatenate(i_parts, axis=-1) + bx_ref[...])
        log_a = -LRU_C * r * softplus_neg_lam
        a = jnp.exp(log_a)
        th = jnp.tanh(log_a)
        mult = jnp.sqrt(-2.0 * th / (1.0 - th))
        mult = jnp.where(pos == 0, 1.0, mult)
        u = mult * (i * xc)
        acc_a, acc_b = a, u
        d = 1
        while d < tt:
            a_sh = _shift_rows(acc_a, d, 1.0)
            b_sh = _shift_rows(acc_b, d, 0.0)
            acc_b = acc_a * b_sh + acc_b
            acc_a = acc_a * a_sh
            d *= 2
        hseq = acc_a * hstate[s] + acc_b
        hstate[s] = hseq[tt - 1:tt, :]
        gate = z_ref[s, :, 0:LRU_WIDTH]
        mix_ref[s, :, 0:LRU_WIDTH] = (_gelu_tanh(gate) * hseq).astype(mix_ref.dtype)
        pin = z_ref[s, :, 2 * LRU_WIDTH:]
        ext[0:16, :] = ptail[s]
        ext[16:16 + tt, :] = pin
        ptail[s] = ext[tt:tt + 16, :]
        for g, w in enumerate(POOL_WINDOWS):
            sl = slice(g * POOL_GROUP_DIM, (g + 1) * POOL_GROUP_DIM)
            win = sum(ext[16 - k:16 - k + tt, sl] for k in range(w))
            cnt = jnp.minimum(pos + 1, w).astype(F32)
            pooled = win / cnt - pin[:, sl]
            bo = jnp.dot(pooled.astype(BF16), pw_ref[g], preferred_element_type=F32) * ps_ref[:, sl]
            mix_ref[s, :, LRU_WIDTH + g * POOL_GROUP_DIM:LRU_WIDTH + (g + 1) * POOL_GROUP_DIM] = bo.astype(mix_ref.dtype)
        return carry

    lax.fori_loop(0, sb, per_seq, 0)

    @pl.when(j == pl.num_programs(1) - 1)
    def _():
        hlast_ref[...] = hstate[...]


def _recmix(z, conv0, h0, pool0, cw, cb, wa, ba, wx, bx, lam, pw, ps, pos0):
    b, t, _ = z.shape
    sb, tt = _seq_tiles(b, t)
    tt = min(tt, 256)
    c = LRU_WIDTH
    row = _const_spec((1, c))
    return pl.pallas_call(
        functools.partial(_recmix_kernel, pos0=pos0),
        grid=(b // sb, t // tt),
        in_specs=[pl.BlockSpec((sb, tt, MIX_IN), lambda i, j: (i, j, 0)),
                  pl.BlockSpec((sb, 8, c), lambda i, j: (i, 0, 0)),
                  pl.BlockSpec((sb, 1, c), lambda i, j: (i, 0, 0)),
                  pl.BlockSpec((sb, 16, c), lambda i, j: (i, 0, 0)),
                  _const_spec((CONV_WIDTH, c)), row,
                  _const_spec((LRU_HEADS, LRU_HEAD_DIM, LRU_HEAD_DIM)), row,
                  _const_spec((LRU_HEADS, LRU_HEAD_DIM, LRU_HEAD_DIM)), row, row,
                  _const_spec((len(POOL_WINDOWS), POOL_GROUP_DIM, POOL_GROUP_DIM)), row],
        out_specs=[pl.BlockSpec((sb, tt, 2 * c), lambda i, j: (i, j, 0)),
                   pl.BlockSpec((sb, 1, c), lambda i, j: (i, 0, 0))],
        out_shape=[jax.ShapeDtypeStruct((b, t, 2 * c), BF16),
                   jax.ShapeDtypeStruct((b, 1, c), F32)],
        scratch_shapes=[pltpu.VMEM((sb, 8, c), F32), pltpu.VMEM((sb, 16, c), F32),
                        pltpu.VMEM((sb, 1, c), F32), pltpu.VMEM((tt + 16, c), F32)],
        compiler_params=_cparams(("arbitrary", "arbitrary")),
        name="l0_recmix",
    )(z, conv0, h0, pool0, cw, cb, wa, ba, wx, bx, lam, pw, ps)


def _proj_res_kernel(a_ref, w_ref, x_ref, g1_ref, n2_ref, sh_ref, sc_ref, xo_ref, ho_ref):
    sb, tt, k = a_ref.shape
    mix = jnp.dot(a_ref[...].reshape(sb * tt, k), w_ref[...], preferred_element_type=F32)
    xn = x_ref[...] + g1_ref[...] * mix.reshape(sb, tt, -1)
    xo_ref[...] = xn
    ho_ref[...] = _norm_mod(xn, n2_ref[...], sh_ref[...], sc_ref[...]).astype(ho_ref.dtype)


def _proj_res(a, w, x, g1, n2, sh2, sc2):
    b, t, d = x.shape
    k = a.shape[-1]
    sb, tt = _seq_tiles(b, t)
    seq = pl.BlockSpec((sb, 1, d), lambda i, j: (i, 0, 0))
    tile = pl.BlockSpec((sb, tt, d), lambda i, j: (i, j, 0))
    return pl.pallas_call(
        _proj_res_kernel,
        grid=(b // sb, t // tt),
        in_specs=[pl.BlockSpec((sb, tt, k), lambda i, j: (i, j, 0)), _const_spec((k, d)), tile,
                  seq, _const_spec((1, d)), seq, seq],
        out_specs=[tile, tile],
        out_shape=[jax.ShapeDtypeStruct((b, t, d), F32), jax.ShapeDtypeStruct((b, t, d), BF16)],
        compiler_params=_cparams(("arbitrary", "arbitrary")),
        name="proj_res_norm",
    )(a, w, x, g1, n2, sh2, sc2)


def _top16_rows(vals):
    tm = vals.shape[1]
    rows16 = lax.broadcasted_iota(jnp.int32, (PEER_TOPK, tm), 0)
    top = jnp.zeros((PEER_TOPK, tm), F32)
    for k in range(PEER_TOPK):
        mx = jnp.max(vals, axis=0, keepdims=True)
        top = jnp.where(rows16 == k, mx, top)
        vals = jnp.where(vals == mx, -jnp.inf, vals)
    return top


def _peer_route_kernel(x_ref, wqt_ref, keys_ref, e1_ref, e2_ref, th_ref, qt_ref):
    qt_ref[...] = lax.dot_general(wqt_ref[...], x_ref[...], (((1,), (1,)), ((), ())),
                                  preferred_element_type=F32).astype(BF16)

    def per_head(h, carry):
        r1 = pl.multiple_of(h * 2 * HALF_KEY, 2 * HALF_KEY)
        r2 = pl.multiple_of(h * 2 * HALF_KEY + HALF_KEY, HALF_KEY)
        s1 = jnp.dot(keys_ref[2 * h], qt_ref[pl.ds(r1, HALF_KEY), :], preferred_element_type=F32)
        s2 = jnp.dot(keys_ref[2 * h + 1], qt_ref[pl.ds(r2, HALF_KEY), :], preferred_element_type=F32)
        t1 = _top16_rows(s1)
        t2 = _top16_rows(s2)
        cand = jnp.concatenate([t1[a:a + 1, :] + t2 for a in range(PEER_TOPK)], axis=0)
        vals = cand
        tau = None
        for _ in range(PEER_TOPK):
            tau = jnp.max(vals, axis=0, keepdims=True)
            vals = jnp.where(vals == tau, -jnp.inf, vals)
        sel = cand >= tau
        m1 = t1[0:1, :]
        m2 = t2[0:1, :]
        z = jnp.sum(jnp.where(sel, jnp.exp(cand - (m1 + m2)), 0.0), axis=0, keepdims=True)
        rz = 1.0 / z
        e1_ref[h] = jnp.exp(s1 - m1) * rz
        e2_ref[h] = jnp.exp(s2 - m2)
        pt1 = jnp.exp(t1 - m1) * rz
        pt2 = jnp.exp(t2 - m2)
        prod = jnp.concatenate([pt1[a:a + 1, :] * pt2 for a in range(PEER_TOPK)], axis=0)
        th_ref[h] = jnp.min(jnp.where(sel, prod, jnp.inf), axis=0, keepdims=True)
        return carry

    lax.fori_loop(0, PEER_HEADS, per_head, 0)


def _peer_route(hn, wqt, keys):
    m, d = hn.shape
    tm = 256
    return pl.pallas_call(
        _peer_route_kernel,
        grid=(m // tm,),
        in_specs=[pl.BlockSpec((tm, d), lambda i: (i, 0)), _const_spec((d, d)),
                  _const_spec((2 * PEER_HEADS, N_KEYS, HALF_KEY))],
        out_specs=[pl.BlockSpec((PEER_HEADS, N_KEYS, tm), lambda i: (0, 0, i)),
                   pl.BlockSpec((PEER_HEADS, N_KEYS, tm), lambda i: (0, 0, i)),
                   pl.BlockSpec((PEER_HEADS, 1, tm), lambda i: (0, 0, i))],
        out_shape=[jax.ShapeDtypeStruct((PEER_HEADS, N_KEYS, m), F32),
                   jax.ShapeDtypeStruct((PEER_HEADS, N_KEYS, m), F32),
                   jax.ShapeDtypeStruct((PEER_HEADS, 1, m), F32)],
        scratch_shapes=[pltpu.VMEM((d, tm), BF16)],
        compiler_params=_cparams(("arbitrary",)),
        name="peer_route",
    )(hn, wqt, keys)


def _gelu_erf(x):
    return 0.5 * x * (1.0 + lax.erf(x * math.sqrt(0.5)))


def _peer_expert_kernel(x_ref, u_ref, vt_ref, e1_ref, e2_ref, th_ref, o_ref, acc_ref, s_ref, a_ref):
    e = pl.program_id(1)
    eb, tm = s_ref.shape

    @pl.when(e == 0)
    def _():
        acc_ref[...] = jnp.zeros_like(acc_ref)

    for sb in range(eb // EXPERT_SUB):
        rows = slice(sb * EXPERT_SUB, (sb + 1) * EXPERT_SUB)
        s_ref[rows, :] = lax.dot_general(u_ref[rows, :], x_ref[...], (((1,), (1,)), ((), ())),
                                         preferred_element_type=F32)
    for ii in range(eb // N_KEYS):
        for jh in range(N_KEYS // GATE_ROWS):
            jrows = slice(jh * GATE_ROWS, (jh + 1) * GATE_ROWS)
            rows = slice(ii * N_KEYS + jh * GATE_ROWS, ii * N_KEYS + (jh + 1) * GATE_ROWS)
            for lt in range(tm // LANES):
                cols = slice(lt * LANES, (lt + 1) * LANES)
                g = jnp.zeros((GATE_ROWS, LANES), F32)
                for h in range(PEER_HEADS):
                    p = e2_ref[h, jrows, cols] * e1_ref[h, ii:ii + 1, cols]
                    g = g + jnp.where(p >= th_ref[h, :, cols], p, 0.0)
                a_ref[rows, cols] = (_gelu_erf(s_ref[rows, cols]) * g).astype(a_ref.dtype)
    acc_ref[...] += jnp.dot(vt_ref[...], a_ref[...], preferred_element_type=F32)

    @pl.when(e == pl.num_programs(1) - 1)
    def _():
        o_ref[...] = acc_ref[...].T


def _peer_experts(hn, u, vt, e1, e2, th):
    m, d = hn.shape
    tm = 512
    eb = 1024
    return pl.pallas_call(
        _peer_expert_kernel,
        grid=(m // tm, N_EXPERTS // eb),
        in_specs=[pl.BlockSpec((tm, d), lambda i, e: (i, 0)),
                  pl.BlockSpec((eb, d), lambda i, e: (e, 0)),
                  pl.BlockSpec((d, eb), lambda i, e: (0, e)),
                  pl.BlockSpec((PEER_HEADS, eb // N_KEYS, tm), lambda i, e: (0, e, i)),
                  pl.BlockSpec((PEER_HEADS, N_KEYS, tm), lambda i, e: (0, 0, i)),
                  pl.BlockSpec((PEER_HEADS, 1, tm), lambda i, e: (0, 0, i))],
        out_specs=pl.BlockSpec((tm, d), lambda i, e: (i, 0)),
        out_shape=jax.ShapeDtypeStruct((m, d), F32),
        scratch_shapes=[pltpu.VMEM((d, tm), F32), pltpu.VMEM((eb, tm), F32), pltpu.VMEM((eb, tm), BF16)],
        compiler_params=_cparams(("arbitrary", "arbitrary")),
        name="peer_experts",
    )(hn, u, vt, e1, e2, th)


def _res_kernel(x_ref, p_ref, g_ref, o_ref):
    o_ref[...] = x_ref[...] + g_ref[...] * p_ref[...]


def _res_norm_kernel(x_ref, p_ref, g_ref, fg_ref, o_ref):
    o_ref[...] = _rmsnorm(x_ref[...] + g_ref[...] * p_ref[...], fg_ref[...])


def _residual(x, p, g2, final_g=None):
    b, t, d = x.shape
    sb, tt = _seq_tiles(b, t)
    tile = pl.BlockSpec((sb, tt, d), lambda i, j: (i, j, 0))
    seq = pl.BlockSpec((sb, 1, d), lambda i, j: (i, 0, 0))
    in_specs = [tile, tile, seq]
    args = [x, p, g2]
    body = _res_kernel
    if final_g is not None:
        in_specs.append(_const_spec((1, d)))
        args.append(final_g)
        body = _res_norm_kernel
    return pl.pallas_call(
        body, grid=(b // sb, t // tt), in_specs=in_specs, out_specs=tile,
        out_shape=jax.ShapeDtypeStruct((b, t, d), F32),
        compiler_params=_cparams(("arbitrary", "arbitrary")),
        name="peer_residual",
    )(*args)


def _qkv_kernel(x_ref, g_ref, sh_ref, sc_ref, wdq_ref, gq_ref, wq1_ref, wq2_ref, wdkv_ref, gkv_ref,
                cos_ref, sin_ref, q_ref, ckv_ref, kpe_ref):
    sb, tt, d = x_ref.shape
    hn = _norm_mod(x_ref[...], g_ref[...], sh_ref[...], sc_ref[...]).reshape(sb * tt, d).astype(BF16)
    cq = (_rmsnorm(jnp.dot(hn, wdq_ref[...], preferred_element_type=F32), gq_ref[...]) * Q_SCALE).astype(BF16)
    q1 = jnp.dot(cq, wq1_ref[...], preferred_element_type=F32)
    q2 = jnp.dot(cq, wq2_ref[...], preferred_element_type=F32)
    cos = cos_ref[...][None]
    sin = sin_ref[...][None]
    for h in range(MLA_HEADS):
        q_ref[:, :, h * QK_PAD:h * QK_PAD + NOPE_DIM] = (
            q1[:, h * QK_PAD:h * QK_PAD + NOPE_DIM].reshape(sb, tt, NOPE_DIM).astype(q_ref.dtype))
        pe = (q1[:, h * QK_PAD + NOPE_DIM:(h + 1) * QK_PAD].reshape(sb, tt, LANES) * cos
              + q2[:, h * LANES:(h + 1) * LANES].reshape(sb, tt, LANES) * sin)
        q_ref[:, :, h * QK_PAD + NOPE_DIM:(h + 1) * QK_PAD] = pe.astype(q_ref.dtype)
    kv = jnp.dot(hn, wdkv_ref[...], preferred_element_type=F32)
    ckv_ref[...] = _rmsnorm(kv[:, :KV_LORA], gkv_ref[...]).reshape(sb, tt, KV_LORA)
    kpe = (kv[:, KV_LORA:KV_LORA + LANES].reshape(sb, tt, LANES) * cos
           + kv[:, KV_LORA + LANES:].reshape(sb, tt, LANES) * sin)
    kpe_ref[...] = kpe[:, :, :ROPE_DIM]


def _qkv(x, g, sh, sc, wdq, gq, wq1, wq2, wdkv, gkv, cos, sin):
    b, t, d = x.shape
    sb, tt = _seq_tiles(b, t)
    tt = min(tt, 256)
    seq = pl.BlockSpec((sb, 1, d), lambda i, j: (i, 0, 0))
    rope = pl.BlockSpec((tt, LANES), lambda i, j: (j, 0))
    return pl.pallas_call(
        _qkv_kernel,
        grid=(b // sb, t // tt),
        in_specs=[pl.BlockSpec((sb, tt, d), lambda i, j: (i, j, 0)), _const_spec((1, d)), seq, seq,
                  _const_spec(wdq.shape), _const_spec((1, Q_LORA)), _const_spec(wq1.shape), _const_spec(wq2.shape),
                  _const_spec(wdkv.shape), _const_spec((1, KV_LORA)), rope, rope],
        out_specs=[pl.BlockSpec((sb, tt, MLA_HEADS * QK_PAD), lambda i, j: (i, j, 0)),
                   pl.BlockSpec((sb, tt, KV_LORA), lambda i, j: (i, j, 0)),
                   pl.BlockSpec((sb, tt, ROPE_DIM), lambda i, j: (i, j, 0))],
        out_shape=[jax.ShapeDtypeStruct((b, t, MLA_HEADS * QK_PAD), BF16),
                   jax.ShapeDtypeStruct((b, t, KV_LORA), F32),
                   jax.ShapeDtypeStruct((b, t, ROPE_DIM), F32)],
        compiler_params=_cparams(("arbitrary", "arbitrary")),
        name="l1_qkv",
    )(x, g, sh, sc, wdq, gq, wq1, wq2, wdkv, gkv, cos, sin)


def _kvup_kernel(ckv_ref, kpe_ref, wk_ref, wv_ref, k_ref, v_ref):
    c = ckv_ref[0].astype(BF16)
    kn = jnp.dot(c, wk_ref[...], preferred_element_type=F32)
    v_ref[0] = jnp.dot(c, wv_ref[...], preferred_element_type=F32).astype(v_ref.dtype)
    kpe = kpe_ref[0].astype(k_ref.dtype)
    for h in range(MLA_HEADS):
        k_ref[0, :, h * QK_PAD:h * QK_PAD + NOPE_DIM] = kn[:, h * NOPE_DIM:(h + 1) * NOPE_DIM].astype(k_ref.dtype)
        k_ref[0, :, h * QK_PAD + NOPE_DIM:(h + 1) * QK_PAD] = kpe


def _kvup(ckv, kpe128, wk, wv):
    b, t, _ = ckv.shape
    tt = max(c for c in range(16, 513, 16) if t % c == 0)
    return pl.pallas_call(
        _kvup_kernel,
        grid=(b, t // tt),
        in_specs=[pl.BlockSpec((1, tt, KV_LORA), lambda i, j: (i, j, 0)),
                  pl.BlockSpec((1, tt, LANES), lambda i, j: (i, j, 0)),
                  _const_spec(wk.shape), _const_spec(wv.shape)],
        out_specs=[pl.BlockSpec((1, tt, MLA_HEADS * QK_PAD), lambda i, j: (i, j, 0)),
                   pl.BlockSpec((1, tt, MLA_HEADS * V_DIM), lambda i, j: (i, j, 0))],
        out_shape=[jax.ShapeDtypeStruct((b, t, MLA_HEADS * QK_PAD), BF16),
                   jax.ShapeDtypeStruct((b, t, MLA_HEADS * V_DIM), BF16)],
        compiler_params=_cparams(("arbitrary", "arbitrary")),
        name="l1_kvup",
    )(ckv, kpe128, wk, wv)


def _qk(q, k):
    return lax.dot_general(q, k, (((1,), (1,)), ((), ())), preferred_element_type=F32)


def _attn_stream_kernel(q_ref, k_ref, v_ref, bias_ref, o_ref, s0, s1, p0, p1, a0, a1, m_run, l_run, acc):
    t = q_ref.shape[1]
    qi = pl.program_id(2)
    q = q_ref[0]

    def scores(k):
        start = pl.multiple_of(jnp.minimum(k, qi) * t, t)
        which = jnp.where(k == qi, 1, jnp.where(k > qi, 2, 0))
        return _qk(q, k_ref[0, pl.ds(start, t), :]) + bias_ref[which]

    def values(k):
        start = pl.multiple_of(jnp.clip(k, 0, qi) * t, t)
        return v_ref[0, pl.ds(start, t), :]

    def half(k, s_cur, s_next, p_cur, p_prev, a_cur, a_prev):
        s_next[...] = scores(k + 1)
        acc[...] = a_prev[...] * acc[...] + jnp.dot(p_prev[...], values(k - 1), preferred_element_type=F32)
        m_old = m_run[...]
        m_new = jnp.maximum(m_old, jnp.max(s_cur[...], axis=-1, keepdims=True))
        alpha = jnp.exp2(m_old - m_new)
        a_cur[...] = alpha
        m_run[...] = m_new
        p = jnp.exp2(s_cur[...] - pltpu.repeat(m_new, t // LANES, axis=1))
        l_run[...] = alpha * l_run[...] + jnp.sum(p, axis=-1, keepdims=True)
        p_cur[...] = p.astype(p_cur.dtype)

    s0[...] = scores(0)
    p1[...] = jnp.zeros(p1.shape, p1.dtype)
    a1[...] = jnp.ones(a1.shape, F32)
    acc[...] = jnp.zeros(acc.shape, F32)
    m_run[...] = jnp.full(m_run.shape, NEG_BIG, F32)
    l_run[...] = jnp.zeros(l_run.shape, F32)

    def body(j, carry):
        half(2 * j, s0, s1, p0, p1, a0, a1)
        half(2 * j + 1, s1, s0, p1, p0, a1, a0)
        return carry

    pairs = (qi + 2) // 2
    lax.fori_loop(0, pairs, body, 0)
    out = a1[...] * acc[...] + jnp.dot(p1[...], values(2 * pairs - 1), preferred_element_type=F32)
    o_ref[0] = (out / l_run[...]).astype(o_ref.dtype)


def _attn_single_kernel(q_ref, k_ref, v_ref, o_ref, *, q_off):
    tq = q_ref.shape[1]
    tk = k_ref.shape[1]
    s = _qk(q_ref[0], k_ref[0])
    qc = (q_off + lax.broadcasted_iota(jnp.int32, (tq, tk), 0)) // CHUNK
    kc = lax.broadcasted_iota(jnp.int32, (tq, tk), 1) // CHUNK
    s = jnp.where(kc <= qc, s, NEG_BIG)
    p = jnp.exp2(s - jnp.max(s, axis=-1, keepdims=True))
    l = jnp.sum(p, axis=-1, keepdims=True)
    o_ref[0] = (jnp.dot(p.astype(BF16), v_ref[0], preferred_element_type=F32) / l).astype(o_ref.dtype)


def _attention(q, k, v):
    b, tq_all, _ = q.shape
    tk_all = k.shape[1]
    out_shape = jax.ShapeDtypeStruct((b, tq_all, MLA_HEADS * V_DIM), BF16)
    kv_specs = [pl.BlockSpec((1, tk_all, QK_PAD), lambda i, h, j: (i, 0, h)),
                pl.BlockSpec((1, tk_all, V_DIM), lambda i, h, j: (i, 0, h))]
    if tq_all == tk_all and tq_all % ATTN_TILE == 0:
        t = ATTN_TILE
        r = lax.broadcasted_iota(jnp.int32, (t, t), 0) // CHUNK
        c = lax.broadcasted_iota(jnp.int32, (t, t), 1) // CHUNK
        bias = jnp.stack([jnp.zeros((t, t), F32), jnp.where(c <= r, 0.0, NEG_BIG).astype(F32),
                          jnp.full((t, t), NEG_BIG, F32)])
        return pl.pallas_call(
            _attn_stream_kernel,
            grid=(b, MLA_HEADS, tq_all // t),
            in_specs=[pl.BlockSpec((1, t, QK_PAD), lambda i, h, j: (i, j, h))] + kv_specs + [_const_spec((3, t, t))],
            out_specs=pl.BlockSpec((1, t, V_DIM), lambda i, h, j: (i, j, h)),
            out_shape=out_shape,
            scratch_shapes=[pltpu.VMEM((t, t), F32), pltpu.VMEM((t, t), F32),
                            pltpu.VMEM((t, t), BF16), pltpu.VMEM((t, t), BF16),
                            pltpu.VMEM((t, LANES), F32), pltpu.VMEM((t, LANES), F32),
                            pltpu.VMEM((t, LANES), F32), pltpu.VMEM((t, LANES), F32), pltpu.VMEM((t, V_DIM), F32)],
            compiler_params=_cparams(("arbitrary", "arbitrary", "arbitrary")),
            name="l1_attention",
        )(q, k, v, bias)
    assert tq_all <= ATTN_TILE, "history + long query block is not a shape this trunk step has"
    return pl.pallas_call(
        functools.partial(_attn_single_kernel, q_off=tk_all - tq_all),
        grid=(b, MLA_HEADS, 1),
        in_specs=[pl.BlockSpec((1, tq_all, QK_PAD), lambda i, h, j: (i, 0, h))] + kv_specs,
        out_specs=pl.BlockSpec((1, tq_all, V_DIM), lambda i, h, j: (i, 0, h)),
        out_shape=out_shape,
        compiler_params=_cparams(("arbitrary", "arbitrary", "arbitrary")),
        name="l1_attention_step",
    )(q, k, v)


def _prep_weights(w):
    p = {}
    p["rec_w_in"] = w["rec_w_in"][0].astype(BF16)
    p["rec_w_a"] = w["rec_w_a"][0].astype(BF16)
    p["rec_w_x"] = w["rec_w_x"][0].astype(BF16)
    p["pool_w"] = w["pool_w"][0].astype(BF16)
    p["rec_w_out"] = w["rec_w_out"][0].astype(BF16)
    p["att_w_dq"] = w["att_w_dq"][0].astype(BF16)
    wuq = w["att_w_uq"][0].reshape(Q_LORA, MLA_HEADS, QK_DIM)
    half = ROPE_DIM // 2
    zpad = jnp.zeros((Q_LORA, MLA_HEADS, QK_PAD - QK_DIM), F32)
    p["wq1"] = jnp.concatenate([wuq, zpad], axis=-1).reshape(Q_LORA, MLA_HEADS * QK_PAD).astype(BF16)
    swapped = jnp.concatenate([wuq[..., NOPE_DIM + half:], wuq[..., NOPE_DIM:NOPE_DIM + half],
                               jnp.zeros((Q_LORA, MLA_HEADS, LANES - ROPE_DIM), F32)], axis=-1)
    p["wq2"] = swapped.reshape(Q_LORA, MLA_HEADS * LANES).astype(BF16)
    wdkv = w["att_w_dkv"][0]
    zl = jnp.zeros((D_MODEL, LANES - ROPE_DIM), F32)
    p["wdkv"] = jnp.concatenate([wdkv, zl, wdkv[:, KV_LORA + half:], wdkv[:, KV_LORA:KV_LORA + half], zl],
                                axis=-1).astype(BF16)
    wukv = w["att_w_ukv"][0].reshape(KV_LORA, MLA_HEADS, NOPE_DIM + V_DIM)
    p["wk"] = wukv[..., :NOPE_DIM].reshape(KV_LORA, MLA_HEADS * NOPE_DIM).astype(BF16)
    p["wv"] = wukv[..., NOPE_DIM:].reshape(KV_LORA, MLA_HEADS * V_DIM).astype(BF16)
    p["att_w_out"] = w["att_w_out"][0].astype(BF16)
    p["peer_wqt"] = [w["peer_w_q"][l].T.astype(BF16) for l in range(2)]
    p["peer_keys"] = [w["peer_keys"][l].reshape(2 * PEER_HEADS, N_KEYS, HALF_KEY).astype(BF16) for l in range(2)]
    p["peer_u"] = [w["peer_u"][l].astype(BF16) for l in range(2)]
    p["peer_vt"] = [w["peer_v"][l].astype(BF16).T for l in range(2)]
    return p


def _rope_tables(pos):
    half = ROPE_DIM // 2
    inv = ROPE_THETA ** (-(jnp.arange(half, dtype=F32) / half))
    ang = pos.astype(F32)[:, None] * inv[None, :]
    cos = jnp.cos(ang)
    sin = jnp.sin(ang)
    z = jnp.zeros((pos.shape[0], LANES - ROPE_DIM), F32)
    return jnp.concatenate([cos, cos, z], axis=-1), jnp.concatenate([-sin, sin, z], axis=-1)


def _peer(hn, p, l):
    b, t, d = hn.shape
    flat = hn.reshape(b * t, d)
    e1, e2, th = _peer_route(flat, p["peer_wqt"][l], p["peer_keys"][l])
    return _peer_experts(flat, p["peer_u"][l], p["peer_vt"][l], e1, e2, th).reshape(b, t, d)


def _trunk(x, mods, pos0, conv_prev, h_prev, pool_prev, ckv_prev, kpe_prev, w, p):
    b, t, d = x.shape
    row = lambda a: a.reshape(1, -1)
    sh1, sc1, g1, sh2, sc2, g2 = mods[0]
    z = _inproj(x, row(w["norm1_g"][0]), sh1, sc1, p["rec_w_in"])
    conv0 = jnp.pad(conv_prev, ((0, 0), (8 - (CONV_WIDTH - 1), 0), (0, 0)))
    pool0 = jnp.pad(pool_prev, ((0, 0), (1, 0), (0, 0)))
    mix, h_last = _recmix(z, conv0, h_prev[:, None, :], pool0, w["rec_conv_w"][0], row(w["rec_conv_b"][0]),
                          p["rec_w_a"], row(w["rec_b_a"][0]), p["rec_w_x"], row(w["rec_b_x"][0]),
                          row(w["rec_lambda"][0]), p["pool_w"], row(w["pool_scale"][0]), pos0)
    new_conv = z[:, t - (CONV_WIDTH - 1):, LRU_WIDTH:2 * LRU_WIDTH]
    new_pool = z[:, t - (POOL_MAX - 1):, 2 * LRU_WIDTH:]
    x, hn = _proj_res(mix, p["rec_w_out"], x, g1, row(w["norm2_g"][0]), sh2, sc2)
    x = _residual(x, _peer(hn, p, 0), g2)
    sh1, sc1, g1, sh2, sc2, g2 = mods[1]
    pos = pos0 + jnp.arange(t, dtype=jnp.int32)
    cos, sin = _rope_tables(pos)
    q, ckv, kpe = _qkv(x, row(w["norm1_g"][1]), sh1, sc1, p["att_w_dq"], row(w["att_g_q"][0]), p["wq1"], p["wq2"],
                       p["wdkv"], row(w["att_g_kv"][0]), cos, sin)
    ckv_all = jnp.concatenate([ckv_prev, ckv], axis=1)
    kpe_all = jnp.concatenate([kpe_prev, kpe], axis=1)
    kpe128 = jnp.pad(kpe_all, ((0, 0), (0, 0), (0, LANES - ROPE_DIM)))
    k, v = _kvup(ckv_all, kpe128, p["wk"], p["wv"])
    o = _attention(q, k, v)
    x, hn = _proj_res(o, p["att_w_out"], x, g1, row(w["norm2_g"][1]), sh2, sc2)
    y = _residual(x, _peer(hn, p, 1), g2, row(w["final_g"]))
    return y, new_conv[None], h_last.reshape(1, b, LRU_WIDTH), new_pool[None], ckv[None], kpe[None]


def kernel(x_prompt, x_sample, c_prompt, c_sample, state_conv, state_lru_h, state_pool, cache_ckv, cache_kpe,
           ada_w, ada_b, norm1_g, norm2_g, rec_w_in, rec_conv_w, rec_conv_b, rec_w_a, rec_b_a, rec_w_x, rec_b_x,
           rec_lambda, pool_w, pool_scale, rec_w_out, att_w_dq, att_g_q, att_w_uq, att_w_dkv, att_g_kv, att_w_ukv,
           att_w_out, peer_w_q, peer_keys, peer_u, peer_v, final_g):
    w = dict(ada_w=ada_w, ada_b=ada_b, norm1_g=norm1_g, norm2_g=norm2_g, rec_w_in=rec_w_in, rec_conv_w=rec_conv_w,
             rec_conv_b=rec_conv_b, rec_w_a=rec_w_a, rec_b_a=rec_b_a, rec_w_x=rec_w_x, rec_b_x=rec_b_x,
             rec_lambda=rec_lambda, pool_w=pool_w, pool_scale=pool_scale, rec_w_out=rec_w_out, att_w_dq=att_w_dq,
             att_g_q=att_g_q, att_w_uq=att_w_uq, att_w_dkv=att_w_dkv, att_g_kv=att_g_kv, att_w_ukv=att_w_ukv,
             att_w_out=att_w_out, peer_w_q=peer_w_q, peer_keys=peer_keys, peer_u=peer_u, peer_v=peer_v,
             final_g=final_g)
    p = _prep_weights(w)
    bp, tp, d = x_prompt.shape
    bs, ts, _ = x_sample.shape
    past = cache_ckv.shape[2]
    depth = ada_w.shape[0]
    rows = bp + bs
    rows_pad = -(-rows // 8) * 8
    c_all = jnp.pad(jnp.concatenate([c_prompt, c_sample], axis=0), ((0, rows_pad - rows), (0, 0)))
    mod = _ada(c_all, ada_w, ada_b)

    def mods_of(lo, n):
        return [[mod[l, lo:lo + n, k * d:(k + 1) * d][:, None, :] for k in range(6)] for l in range(depth)]

    zeros = lambda *s: jnp.zeros(s, F32)
    out_p = _trunk(x_prompt, mods_of(0, bp), 0, zeros(bp, CONV_WIDTH - 1, LRU_WIDTH), zeros(bp, LRU_WIDTH),
                   zeros(bp, POOL_MAX - 1, POOL_WIDTH), zeros(bp, 0, KV_LORA), zeros(bp, 0, ROPE_DIM), w, p)
    out_s = _trunk(x_sample, mods_of(bp, bs), past, state_conv[0], state_lru_h[0], state_pool[0],
                   cache_ckv[0], cache_kpe[0], w, p)
    return (out_p[0], out_s[0], out_p[1], out_s[1], out_p[2], out_s[2],
            out_p[3], out_s[3], out_p[4], out_s[4], out_p[5], out_s[5])
```

```python
import functools
import math

import jax
import jax.numpy as jnp
from jax import lax
from jax.experimental import pallas as pl
from jax.experimental.pallas import tpu as pltpu

F32 = jnp.float32
BF16 = jnp.bfloat16

D_MODEL = 2048
CHUNK = 64
RMS_EPS = 1e-6
LRU_WIDTH = 1024
LRU_HEADS = 8
LRU_HEAD_DIM = 128
CONV_WIDTH = 4
LRU_C = 8.0
POOL_WIDTH = 1024
POOL_WINDOWS = (2, 4, 8, 16)
POOL_GROUP_DIM = 256
POOL_MAX = 16
MIX_IN = 3072
MLA_HEADS = 16
Q_LORA = 512
KV_LORA = 256
NOPE_DIM = 128
ROPE_DIM = 64
V_DIM = 128
QK_DIM = NOPE_DIM + ROPE_DIM
ROPE_THETA = 10000.0
PEER_HEADS = 8
N_KEYS = 128
N_EXPERTS = N_KEYS * N_KEYS
PEER_TOPK = 16
HALF_KEY = 128

LANES = 128
GATE_ROWS = 64
EXPERT_BLOCK = 1024
EXPERT_SUB = 256
QK_PAD = 256
ATTN_TILE = 512
SOFTMAX_ROWS = 64
Q_SCALE = QK_DIM ** -0.5 * math.log2(math.e)
NEG_BIG = -1e30
VMEM_LIMIT = 56 * 1024 * 1024


def _cparams(sem):
    return pltpu.CompilerParams(dimension_semantics=sem, vmem_limit_bytes=VMEM_LIMIT)


def _const_spec(shape):
    n = len(shape)
    return pl.BlockSpec(shape, lambda *_: (0,) * n)


def _seq_tiles(b, t):
    if t >= 512:
        return 1, 512
    sb = max(1, min(b, 512 // t))
    while b % sb:
        sb -= 1
    return sb, t


def _rmsnorm(x, g):
    return x * lax.rsqrt(jnp.mean(x * x, axis=-1, keepdims=True) + RMS_EPS) * g


def _norm_mod(x, g, sh, sc):
    return _rmsnorm(x, g) * (1.0 + sc) + sh


def _ada_kernel(c_ref, w_ref, b_ref, o_ref):
    c = c_ref[...]
    sc = (c * jax.nn.sigmoid(c)).astype(BF16)
    o_ref[0] = jnp.dot(sc, w_ref[0].astype(BF16), preferred_element_type=F32) + b_ref[0]


def _ada(c_all, ada_w, ada_b):
    depth, d, n = ada_w.shape
    rows = c_all.shape[0]
    tn = 1024
    return pl.pallas_call(
        _ada_kernel,
        grid=(depth, n // tn),
        in_specs=[pl.BlockSpec((rows, d), lambda l, j: (0, 0)),
                  pl.BlockSpec((1, d, tn), lambda l, j: (l, 0, j)),
                  pl.BlockSpec((1, 1, tn), lambda l, j: (l, 0, j))],
        out_specs=pl.BlockSpec((1, rows, tn), lambda l, j: (l, 0, j)),
        out_shape=jax.ShapeDtypeStruct((depth, rows, n), F32),
        compiler_params=_cparams(("arbitrary", "arbitrary")),
        name="ada_mod",
    )(c_all, ada_w, ada_b.reshape(depth, 1, n))


def _inproj_kernel(x_ref, g_ref, sh_ref, sc_ref, w_ref, o_ref):
    sb, tt, d = x_ref.shape
    hn = _norm_mod(x_ref[...], g_ref[...], sh_ref[...], sc_ref[...])
    z = jnp.dot(hn.reshape(sb * tt, d).astype(BF16), w_ref[...], preferred_element_type=F32)
    o_ref[...] = z.reshape(sb, tt, -1)


def _inproj(x, g, sh, sc, w):
    b, t, d = x.shape
    n = w.shape[1]
    sb, tt = _seq_tiles(b, t)
    tt = min(tt, 256)
    seq = pl.BlockSpec((sb, 1, d), lambda i, j: (i, 0, 0))
    return pl.pallas_call(
        _inproj_kernel,
        grid=(b // sb, t // tt),
        in_specs=[pl.BlockSpec((sb, tt, d), lambda i, j: (i, j, 0)),
                  _const_spec((1, d)), seq, seq, _const_spec((d, n))],
        out_specs=pl.BlockSpec((sb, tt, n), lambda i, j: (i, j, 0)),
        out_shape=jax.ShapeDtypeStruct((b, t, n), F32),
        compiler_params=_cparams(("arbitrary", "arbitrary")),
        name="l0_inproj",
    )(x, g, sh, sc, w)


def _shift_rows(x, d, fill):
    rolled = pltpu.roll(x, d, 0)
    rows = lax.broadcasted_iota(jnp.int32, x.shape, 0)
    return jnp.where(rows >= d, rolled, fill)


def _gelu_tanh(x):
    return 0.5 * x * (1.0 + jnp.tanh(math.sqrt(2.0 / math.pi) * (x + 0.044715 * (x * x * x))))


def _recmix_kernel(z_ref, conv0_ref, h0_ref, pool0_ref, cw_ref, cb_ref, wa_ref, ba_ref, wx_ref, bx_ref,
                   lam_ref, pw_ref, ps_ref, mix_ref, hlast_ref, ctail, ptail, hstate, ext, *, pos0):
    sb, tt, _ = z_ref.shape
    j = pl.program_id(1)

    @pl.when(j == 0)
    def _():
        ctail[...] = conv0_ref[...]
        ptail[...] = pool0_ref[...]
        hstate[...] = h0_ref[...]

    lam = lam_ref[...]
    neg = -lam
    softplus_neg_lam = jnp.maximum(neg, 0.0) + jnp.log1p(jnp.exp(-jnp.abs(neg)))
    pos = pos0 + j * tt + lax.broadcasted_iota(jnp.int32, (tt, 1), 0)

    def per_seq(s, carry):
        rec = z_ref[s, :, LRU_WIDTH:2 * LRU_WIDTH]
        ext[0:8, :] = ctail[s]
        ext[8:8 + tt, :] = rec
        xc = cb_ref[...] + sum(ext[5 + k:5 + k + tt, :] * cw_ref[k:k + 1, :] for k in range(CONV_WIDTH))
        ctail[s] = ext[tt:tt + 8, :]
        xcb = xc.astype(BF16)
        r_parts, i_parts = [], []
        for h in range(LRU_HEADS):
            sl = slice(h * LRU_HEAD_DIM, (h + 1) * LRU_HEAD_DIM)
            r_parts.append(jnp.dot(xcb[:, sl], wa_ref[h], preferred_element_type=F32))
            i_parts.append(jnp.dot(xcb[:, sl], wx_ref[h], preferred_element_type=F32))
        r = jax.nn.sigmoid(jnp.concatenate(r_parts, axis=-1) + ba_ref[...])
        i = jax.nn.sigmoid(jnp.concatenate(i_parts, axis=-1) + bx_ref[...])
        log_a = -LRU_C * r * softplus_neg_lam
        a = jnp.exp(log_a)
        th = jnp.tanh(log_a)
        mult = jnp.sqrt(-2.0 * th / (1.0 - th))
        mult = jnp.where(pos == 0, 1.0, mult)
        u = mult * (i * xc)
        acc_a, acc_b = a, u
        d = 1
        while d < tt:
            a_sh = _shift_rows(acc_a, d, 1.0)
            b_sh = _shift_rows(acc_b, d, 0.0)
            acc_b = acc_a * b_sh + acc_b
            acc_a = acc_a * a_sh
            d *= 2
        hseq = acc_a * hstate[s] + acc_b
        hstate[s] = hseq[tt - 1:tt, :]
        gate = z_ref[s, :, 0:LRU_WIDTH]
        mix_ref[s, :, 0:LRU_WIDTH] = (_gelu_tanh(gate) * hseq).astype(mix_ref.dtype)
        pin = z_ref[s, :, 2 * LRU_WIDTH:]
        ext[0:16, :] = ptail[s]
        ext[16:16 + tt, :] = pin
        ptail[s] = ext[tt:tt + 16, :]
        for g, w in enumerate(POOL_WINDOWS):
            sl = slice(g * POOL_GROUP_DIM, (g + 1) * POOL_GROUP_DIM)
            win = sum(ext[16 - k:16 - k + tt, sl] for k in range(w))
            cnt = jnp.minimum(pos + 1, w).astype(F32)
            pooled = win / cnt - pin[:, sl]
            bo = jnp.dot(pooled.astype(BF16), pw_ref[g], preferred_element_type=F32) * ps_ref[:, sl]
            mix_ref[s, :, LRU_WIDTH + g * POOL_GROUP_DIM:LRU_WIDTH + (g + 1) * POOL_GROUP_DIM] = bo.astype(mix_ref.dtype)
        return carry

    lax.fori_loop(0, sb, per_seq, 0)

    @pl.when(j == pl.num_programs(1) - 1)
    def _():
        hlast_ref[...] = hstate[...]


def _recmix(z, conv0, h0, pool0, cw, cb, wa, ba, wx, bx, lam, pw, ps, pos0):
    b, t, _ = z.shape
    sb, tt = _seq_tiles(b, t)
    tt = min(tt, 256)
    c = LRU_WIDTH
    row = _const_spec((1, c))
    return pl.pallas_call(
        functools.partial(_recmix_kernel, pos0=pos0),
        grid=(b // sb, t // tt),
        in_specs=[pl.BlockSpec((sb, tt, MIX_IN), lambda i, j: (i, j, 0)),
                  pl.BlockSpec((sb, 8, c), lambda i, j: (i, 0, 0)),
                  pl.BlockSpec((sb, 1, c), lambda i, j: (i, 0, 0)),
                  pl.BlockSpec((sb, 16, c), lambda i, j: (i, 0, 0)),
                  _const_spec((CONV_WIDTH, c)), row,
                  _const_spec((LRU_HEADS, LRU_HEAD_DIM, LRU_HEAD_DIM)), row,
                  _const_spec((LRU_HEADS, LRU_HEAD_DIM, LRU_HEAD_DIM)), row, row,
                  _const_spec((len(POOL_WINDOWS), POOL_GROUP_DIM, POOL_GROUP_DIM)), row],
        out_specs=[pl.BlockSpec((sb, tt, 2 * c), lambda i, j: (i, j, 0)),
                   pl.BlockSpec((sb, 1, c), lambda i, j: (i, 0, 0))],
        out_shape=[jax.ShapeDtypeStruct((b, t, 2 * c), BF16),
                   jax.ShapeDtypeStruct((b, 1, c), F32)],
        scratch_shapes=[pltpu.VMEM((sb, 8, c), F32), pltpu.VMEM((sb, 16, c), F32),
                        pltpu.VMEM((sb, 1, c), F32), pltpu.VMEM((tt + 16, c), F32)],
        compiler_params=_cparams(("arbitrary", "arbitrary")),
        name="l0_recmix",
    )(z, conv0, h0, pool0, cw, cb, wa, ba, wx, bx, lam, pw, ps)


def _proj_res_kernel(a_ref, w_ref, x_ref, g1_ref, n2_ref, sh_ref, sc_ref, xo_ref, ho_ref):
    sb, tt, k = a_ref.shape
    mix = jnp.dot(a_ref[...].reshape(sb * tt, k), w_ref[...], preferred_element_type=F32)
    xn = x_ref[...] + g1_ref[...] * mix.reshape(sb, tt, -1)
    xo_ref[...] = xn
    ho_ref[...] = _norm_mod(xn, n2_ref[...], sh_ref[...], sc_ref[...]).astype(ho_ref.dtype)


def _proj_res(a, w, x, g1, n2, sh2, sc2):
    b, t, d = x.shape
    k = a.shape[-1]
    sb, tt = _seq_tiles(b, t)
    seq = pl.BlockSpec((sb, 1, d), lambda i, j: (i, 0, 0))
    tile = pl.BlockSpec((sb, tt, d), lambda i, j: (i, j, 0))
    return pl.pallas_call(
        _proj_res_kernel,
        grid=(b // sb, t // tt),
        in_specs=[pl.BlockSpec((sb, tt, k), lambda i, j: (i, j, 0)), _const_spec((k, d)), tile,
                  seq, _const_spec((1, d)), seq, seq],
        out_specs=[tile, tile],
        out_shape=[jax.ShapeDtypeStruct((b, t, d), F32), jax.ShapeDtypeStruct((b, t, d), BF16)],
        compiler_params=_cparams(("arbitrary", "arbitrary")),
        name="proj_res_norm",
    )(a, w, x, g1, n2, sh2, sc2)


def _top16_rows(vals):
    tm = vals.shape[1]
    rows16 = lax.broadcasted_iota(jnp.int32, (PEER_TOPK, tm), 0)
    top = jnp.zeros((PEER_TOPK, tm), F32)
    for k in range(PEER_TOPK):
        mx = jnp.max(vals, axis=0, keepdims=True)
        top = jnp.where(rows16 == k, mx, top)
        vals = jnp.where(vals == mx, -jnp.inf, vals)
    return top


def _rank_pairs(t1, t2, op, fill):
    tm = t1.shape[1]
    row = lax.broadcasted_iota(jnp.int32, (PEER_TOPK, tm), 0)

    def put(slab, val, r0, n):
        return jnp.where(row < r0, slab, jnp.where(row < r0 + n, val, slab))

    def a_with_prefix(slab, a, r0, n):
        t2s = pltpu.roll(t2, r0, 0) if r0 else t2
        return put(slab, op(t1[a:a + 1, :], t2s), r0, n)

    blank = jnp.full((PEER_TOPK, tm), fill, F32)
    slab0 = op(t1[0:1, :], t2)
    slab1 = a_with_prefix(a_with_prefix(a_with_prefix(blank, 1, 0, 8), 2, 8, 5), 4, 13, 3)
    slab2 = blank
    for a, r0, n in ((3, 0, 4), (5, 4, 2), (6, 6, 2), (7, 8, 2)):
        slab2 = a_with_prefix(slab2, a, r0, n)
    tail = op(pltpu.roll(t1, 2, 0), t2[0:1, :])
    slab2 = put(slab2, tail, 10, 6)
    slab3 = put(blank, tail, 0, 2)
    return jnp.concatenate([slab0, slab1, slab2, slab3], axis=0)


def _peer_route_kernel(x_ref, wqt_ref, keys_ref, e1_ref, e2_ref, th_ref, qt_ref):
    qt_ref[...] = lax.dot_general(wqt_ref[...], x_ref[...], (((1,), (1,)), ((), ())),
                                  preferred_element_type=F32).astype(BF16)

    def per_head(h):
        r1 = pl.multiple_of(h * 2 * HALF_KEY, 2 * HALF_KEY)
        r2 = pl.multiple_of(h * 2 * HALF_KEY + HALF_KEY, HALF_KEY)
        s1 = jnp.dot(keys_ref[2 * h], qt_ref[pl.ds(r1, HALF_KEY), :], preferred_element_type=F32)
        s2 = jnp.dot(keys_ref[2 * h + 1], qt_ref[pl.ds(r2, HALF_KEY), :], preferred_element_type=F32)
        t1 = _top16_rows(s1)
        t2 = _top16_rows(s2)
        cand = _rank_pairs(t1, t2, jnp.add, -jnp.inf)
        vals = cand
        tau = None
        for _ in range(PEER_TOPK):
            tau = jnp.max(vals, axis=0, keepdims=True)
            vals = jnp.where(vals == tau, -jnp.inf, vals)
        sel = cand >= tau
        m1 = t1[0:1, :]
        m2 = t2[0:1, :]
        z = jnp.sum(jnp.where(sel, jnp.exp(cand - (m1 + m2)), 0.0), axis=0, keepdims=True)
        rz = 1.0 / z
        e1_ref[h] = jnp.exp(s1 - m1) * rz
        e2_ref[h] = jnp.exp(s2 - m2)
        prod = _rank_pairs(jnp.exp(t1 - m1) * rz, jnp.exp(t2 - m2), jnp.multiply, jnp.inf)
        th_ref[h] = jnp.min(jnp.where(sel, prod, jnp.inf), axis=0, keepdims=True)

    def head_pair(hp, carry):
        per_head(2 * hp)
        per_head(2 * hp + 1)
        return carry

    lax.fori_loop(0, PEER_HEADS // 2, head_pair, 0)


def _peer_route(hn, wqt, keys):
    m, d = hn.shape
    tm = 256
    return pl.pallas_call(
        _peer_route_kernel,
        grid=(m // tm,),
        in_specs=[pl.BlockSpec((tm, d), lambda i: (i, 0)), _const_spec((d, d)),
                  _const_spec((2 * PEER_HEADS, N_KEYS, HALF_KEY))],
        out_specs=[pl.BlockSpec((PEER_HEADS, N_KEYS, tm), lambda i: (0, 0, i)),
                   pl.BlockSpec((PEER_HEADS, N_KEYS, tm), lambda i: (0, 0, i)),
                   pl.BlockSpec((PEER_HEADS, 1, tm), lambda i: (0, 0, i))],
        out_shape=[jax.ShapeDtypeStruct((PEER_HEADS, N_KEYS, m), F32),
                   jax.ShapeDtypeStruct((PEER_HEADS, N_KEYS, m), F32),
                   jax.ShapeDtypeStruct((PEER_HEADS, 1, m), F32)],
        scratch_shapes=[pltpu.VMEM((d, tm), BF16)],
        compiler_params=_cparams(("arbitrary",)),
        name="peer_route",
    )(hn, wqt, keys)


def _gelu_erf(x):
    return 0.5 * x * (1.0 + lax.erf(x * math.sqrt(0.5)))


def _peer_expert_kernel(h_ref, u_ref, vt_ref, e1_ref, e2_ref, th_ref, x_ref, g2_ref, *rest, final_norm):
    if final_norm:
        fg_ref, o_ref, acc_ref, s_ref, a_ref = rest
    else:
        o_ref, acc_ref, s_ref, a_ref = rest
    e = pl.program_id(2)
    eb, tm = s_ref.shape
    sb_, tt, d = h_ref.shape
    hb = h_ref[...].reshape(tm, d)

    @pl.when(e == 0)
    def _():
        acc_ref[...] = jnp.zeros_like(acc_ref)

    for sb in range(eb // EXPERT_SUB):
        rows = slice(sb * EXPERT_SUB, (sb + 1) * EXPERT_SUB)
        s_ref[rows, :] = lax.dot_general(u_ref[rows, :], hb, (((1,), (1,)), ((), ())),
                                         preferred_element_type=F32)
    for ii in range(eb // N_KEYS):
        for jh in range(N_KEYS // GATE_ROWS):
            jrows = slice(jh * GATE_ROWS, (jh + 1) * GATE_ROWS)
            rows = slice(ii * N_KEYS + jh * GATE_ROWS, ii * N_KEYS + (jh + 1) * GATE_ROWS)
            for lt in range(tm // LANES):
                cols = slice(lt * LANES, (lt + 1) * LANES)
                g = jnp.zeros((GATE_ROWS, LANES), F32)
                for h in range(PEER_HEADS):
                    p = e2_ref[h, jrows, cols] * e1_ref[h, ii:ii + 1, cols]
                    g = g + jnp.where(p >= th_ref[h, :, cols], p, 0.0)
                a_ref[rows, cols] = (_gelu_erf(s_ref[rows, cols]) * g).astype(a_ref.dtype)
    acc_ref[...] += jnp.dot(vt_ref[0], a_ref[...], preferred_element_type=F32)

    @pl.when(e == pl.num_programs(2) - 1)
    def _():
        y = x_ref[...] + g2_ref[...] * acc_ref[...].T.reshape(sb_, tt, d)
        o_ref[...] = _rmsnorm(y, fg_ref[...]) if final_norm else y


def _peer_experts(hn, u, vt, e1, e2, th, x, g2, final_g):
    b, t, d = hn.shape
    sb, tt = _seq_tiles(b, t)
    tm = sb * tt
    nt = t // tt
    eb = vt.shape[2]
    tile = pl.BlockSpec((sb, tt, d), lambda i, j, e: (i, j, 0))
    in_specs = [tile,
                pl.BlockSpec((eb, d), lambda i, j, e: (e, 0)),
                pl.BlockSpec((1, d, eb), lambda i, j, e: (e, 0, 0)),
                pl.BlockSpec((PEER_HEADS, eb // N_KEYS, tm), lambda i, j, e: (0, e, i * nt + j)),
                pl.BlockSpec((PEER_HEADS, N_KEYS, tm), lambda i, j, e: (0, 0, i * nt + j)),
                pl.BlockSpec((PEER_HEADS, 1, tm), lambda i, j, e: (0, 0, i * nt + j)),
                tile,
                pl.BlockSpec((sb, 1, d), lambda i, j, e: (i, 0, 0))]
    args = [hn, u, vt, e1, e2, th, x, g2]
    if final_g is not None:
        in_specs.append(_const_spec((1, d)))
        args.append(final_g)
    return pl.pallas_call(
        functools.partial(_peer_expert_kernel, final_norm=final_g is not None),
        grid=(b // sb, nt, N_EXPERTS // eb),
        in_specs=in_specs,
        out_specs=tile,
        out_shape=jax.ShapeDtypeStruct((b, t, d), F32),
        scratch_shapes=[pltpu.VMEM((d, tm), F32), pltpu.VMEM((eb, tm), F32), pltpu.VMEM((eb, tm), BF16)],
        compiler_params=_cparams(("arbitrary", "arbitrary", "arbitrary")),
        name="peer_experts",
    )(*args)


def _qkv_kernel(x_ref, g_ref, sh_ref, sc_ref, wdq_ref, gq_ref, wq1_ref, wq2_ref, wdkv_ref, gkv_ref,
                cos_ref, sin_ref, q_ref, ckv_ref, kpe_ref):
    sb, tt, d = x_ref.shape
    hn = _norm_mod(x_ref[...], g_ref[...], sh_ref[...], sc_ref[...]).reshape(sb * tt, d).astype(BF16)
    cq = (_rmsnorm(jnp.dot(hn, wdq_ref[...], preferred_element_type=F32), gq_ref[...]) * Q_SCALE).astype(BF16)
    q1 = jnp.dot(cq, wq1_ref[...], preferred_element_type=F32)
    q2 = jnp.dot(cq, wq2_ref[...], preferred_element_type=F32)
    cos = cos_ref[...][None]
    sin = sin_ref[...][None]
    for h in range(MLA_HEADS):
        q_ref[:, :, h * QK_PAD:h * QK_PAD + NOPE_DIM] = (
            q1[:, h * QK_PAD:h * QK_PAD + NOPE_DIM].reshape(sb, tt, NOPE_DIM).astype(q_ref.dtype))
        pe = (q1[:, h * QK_PAD + NOPE_DIM:(h + 1) * QK_PAD].reshape(sb, tt, LANES) * cos
              + q2[:, h * LANES:(h + 1) * LANES].reshape(sb, tt, LANES) * sin)
        q_ref[:, :, h * QK_PAD + NOPE_DIM:(h + 1) * QK_PAD] = pe.astype(q_ref.dtype)
    kv = jnp.dot(hn, wdkv_ref[...], preferred_element_type=F32)
    ckv_ref[...] = _rmsnorm(kv[:, :KV_LORA], gkv_ref[...]).reshape(sb, tt, KV_LORA)
    kpe = (kv[:, KV_LORA:KV_LORA + LANES].reshape(sb, tt, LANES) * cos
           + kv[:, KV_LORA + LANES:].reshape(sb, tt, LANES) * sin)
    kpe_ref[...] = kpe[:, :, :ROPE_DIM]


def _qkv(x, g, sh, sc, wdq, gq, wq1, wq2, wdkv, gkv, cos, sin):
    b, t, d = x.shape
    sb, tt = _seq_tiles(b, t)
    tt = min(tt, 256)
    seq = pl.BlockSpec((sb, 1, d), lambda i, j: (i, 0, 0))
    rope = pl.BlockSpec((tt, LANES), lambda i, j: (j, 0))
    return pl.pallas_call(
        _qkv_kernel,
        grid=(b // sb, t // tt),
        in_specs=[pl.BlockSpec((sb, tt, d), lambda i, j: (i, j, 0)), _const_spec((1, d)), seq, seq,
                  _const_spec(wdq.shape), _const_spec((1, Q_LORA)), _const_spec(wq1.shape), _const_spec(wq2.shape),
                  _const_spec(wdkv.shape), _const_spec((1, KV_LORA)), rope, rope],
        out_specs=[pl.BlockSpec((sb, tt, MLA_HEADS * QK_PAD), lambda i, j: (i, j, 0)),
                   pl.BlockSpec((sb, tt, KV_LORA), lambda i, j: (i, j, 0)),
                   pl.BlockSpec((sb, tt, ROPE_DIM), lambda i, j: (i, j, 0))],
        out_shape=[jax.ShapeDtypeStruct((b, t, MLA_HEADS * QK_PAD), BF16),
                   jax.ShapeDtypeStruct((b, t, KV_LORA), F32),
                   jax.ShapeDtypeStruct((b, t, ROPE_DIM), F32)],
        compiler_params=_cparams(("arbitrary", "arbitrary")),
        name="l1_qkv",
    )(x, g, sh, sc, wdq, gq, wq1, wq2, wdkv, gkv, cos, sin)


def _kvup_kernel(ckv_ref, kpe_ref, wk_ref, wv_ref, k_ref, v_ref):
    c = ckv_ref[0].astype(BF16)
    kn = jnp.dot(c, wk_ref[...], preferred_element_type=F32)
    v_ref[0] = jnp.dot(c, wv_ref[...], preferred_element_type=F32).astype(v_ref.dtype)
    kpe = kpe_ref[0].astype(k_ref.dtype)
    for h in range(MLA_HEADS):
        k_ref[0, :, h * QK_PAD:h * QK_PAD + NOPE_DIM] = kn[:, h * NOPE_DIM:(h + 1) * NOPE_DIM].astype(k_ref.dtype)
        k_ref[0, :, h * QK_PAD + NOPE_DIM:(h + 1) * QK_PAD] = kpe


def _kvup(ckv, kpe128, wk, wv):
    b, t, _ = ckv.shape
    tt = max(c for c in range(16, 513, 16) if t % c == 0)
    return pl.pallas_call(
        _kvup_kernel,
        grid=(b, t // tt),
        in_specs=[pl.BlockSpec((1, tt, KV_LORA), lambda i, j: (i, j, 0)),
                  pl.BlockSpec((1, tt, LANES), lambda i, j: (i, j, 0)),
                  _const_spec(wk.shape), _const_spec(wv.shape)],
        out_specs=[pl.BlockSpec((1, tt, MLA_HEADS * QK_PAD), lambda i, j: (i, j, 0)),
                   pl.BlockSpec((1, tt, MLA_HEADS * V_DIM), lambda i, j: (i, j, 0))],
        out_shape=[jax.ShapeDtypeStruct((b, t, MLA_HEADS * QK_PAD), BF16),
                   jax.ShapeDtypeStruct((b, t, MLA_HEADS * V_DIM), BF16)],
        compiler_params=_cparams(("arbitrary", "arbitrary")),
        name="l1_kvup",
    )(ckv, kpe128, wk, wv)


def _qk(q, k):
    return lax.dot_general(q, k, (((1,), (1,)), ((), ())), preferred_element_type=F32)


def _attn_stream_kernel(q_ref, k_ref, v_ref, bias_ref, o_ref, s0, s1, p0, p1, a0, a1, m_run, l_run, acc):
    t = q_ref.shape[1]
    qi = pl.program_id(2)
    q = q_ref[0]

    def scores(k):
        start = pl.multiple_of(jnp.minimum(k, qi) * t, t)
        which = jnp.where(k == qi, 1, jnp.where(k > qi, 2, 0))
        return _qk(q, k_ref[0, pl.ds(start, t), :]) + bias_ref[which]

    def values(k):
        start = pl.multiple_of(jnp.clip(k, 0, qi) * t, t)
        return v_ref[0, pl.ds(start, t), :]

    def half(k, s_cur, s_next, p_cur, p_prev, a_cur, a_prev):
        s_next[...] = scores(k + 1)
        acc[...] = a_prev[...] * acc[...] + jnp.dot(p_prev[...], values(k - 1), preferred_element_type=F32)
        m_old = m_run[...]
        m_new = jnp.maximum(m_old, jnp.max(s_cur[...], axis=-1, keepdims=True))
        alpha = jnp.exp2(m_old - m_new)
        a_cur[...] = alpha
        m_run[...] = m_new
        p = jnp.exp2(s_cur[...] - jnp.concatenate([m_new] * (t // LANES), axis=1))
        l_run[...] = alpha * l_run[...] + jnp.sum(p, axis=-1, keepdims=True)
        p_cur[...] = p.astype(p_cur.dtype)

    s0[...] = scores(0)
    p1[...] = jnp.zeros(p1.shape, p1.dtype)
    a1[...] = jnp.ones(a1.shape, F32)
    acc[...] = jnp.zeros(acc.shape, F32)
    m_run[...] = jnp.full(m_run.shape, NEG_BIG, F32)
    l_run[...] = jnp.zeros(l_run.shape, F32)

    def body(j, carry):
        half(2 * j, s0, s1, p0, p1, a0, a1)
        half(2 * j + 1, s1, s0, p1, p0, a1, a0)
        return carry

    pairs = (qi + 2) // 2
    lax.fori_loop(0, pairs, body, 0)
    out = a1[...] * acc[...] + jnp.dot(p1[...], values(2 * pairs - 1), preferred_element_type=F32)
    o_ref[0] = (out / l_run[...]).astype(o_ref.dtype)


def _attn_single_kernel(q_ref, k_ref, v_ref, o_ref, *, q_off):
    tq = q_ref.shape[1]
    tk = k_ref.shape[1]
    s = _qk(q_ref[0], k_ref[0])
    qc = (q_off + lax.broadcasted_iota(jnp.int32, (tq, tk), 0)) // CHUNK
    kc = lax.broadcasted_iota(jnp.int32, (tq, tk), 1) // CHUNK
    s = jnp.where(kc <= qc, s, NEG_BIG)
    p = jnp.exp2(s - jnp.max(s, axis=-1, keepdims=True))
    l = jnp.sum(p, axis=-1, keepdims=True)
    o_ref[0] = (jnp.dot(p.astype(BF16), v_ref[0], preferred_element_type=F32) / l).astype(o_ref.dtype)


def _attention(q, k, v):
    b, tq_all, _ = q.shape
    tk_all = k.shape[1]
    out_shape = jax.ShapeDtypeStruct((b, tq_all, MLA_HEADS * V_DIM), BF16)
    kv_specs = [pl.BlockSpec((1, tk_all, QK_PAD), lambda i, h, j: (i, 0, h)),
                pl.BlockSpec((1, tk_all, V_DIM), lambda i, h, j: (i, 0, h))]
    if tq_all == tk_all and tq_all % ATTN_TILE == 0:
        t = ATTN_TILE
        r = lax.broadcasted_iota(jnp.int32, (t, t), 0) // CHUNK
        c = lax.broadcasted_iota(jnp.int32, (t, t), 1) // CHUNK
        bias = jnp.stack([jnp.zeros((t, t), F32), jnp.where(c <= r, 0.0, NEG_BIG).astype(F32),
                          jnp.full((t, t), NEG_BIG, F32)])
        return pl.pallas_call(
            _attn_stream_kernel,
            grid=(b, MLA_HEADS, tq_all // t),
            in_specs=[pl.BlockSpec((1, t, QK_PAD), lambda i, h, j: (i, j, h))] + kv_specs + [_const_spec((3, t, t))],
            out_specs=pl.BlockSpec((1, t, V_DIM), lambda i, h, j: (i, j, h)),
            out_shape=out_shape,
            scratch_shapes=[pltpu.VMEM((t, t), F32), pltpu.VMEM((t, t), F32),
                            pltpu.VMEM((t, t), BF16), pltpu.VMEM((t, t), BF16),
                            pltpu.VMEM((t, LANES), F32), pltpu.VMEM((t, LANES), F32),
                            pltpu.VMEM((t, LANES), F32), pltpu.VMEM((t, LANES), F32), pltpu.VMEM((t, V_DIM), F32)],
            compiler_params=_cparams(("arbitrary", "arbitrary", "arbitrary")),
            name="l1_attention",
        )(q, k, v, bias)
    assert tq_all <= ATTN_TILE, "history + long query block is not a shape this trunk step has"
    return pl.pallas_call(
        functools.partial(_attn_single_kernel, q_off=tk_all - tq_all),
        grid=(b, MLA_HEADS, 1),
        in_specs=[pl.BlockSpec((1, tq_all, QK_PAD), lambda i, h, j: (i, 0, h))] + kv_specs,
        out_specs=pl.BlockSpec((1, tq_all, V_DIM), lambda i, h, j: (i, 0, h)),
        out_shape=out_shape,
        compiler_params=_cparams(("arbitrary", "arbitrary", "arbitrary")),
        name="l1_attention_step",
    )(q, k, v)


def _prep_weights(w):
    p = {}
    p["rec_w_in"] = w["rec_w_in"][0].astype(BF16)
    p["rec_w_a"] = w["rec_w_a"][0].astype(BF16)
    p["rec_w_x"] = w["rec_w_x"][0].astype(BF16)
    p["pool_w"] = w["pool_w"][0].astype(BF16)
    p["rec_w_out"] = w["rec_w_out"][0].astype(BF16)
    p["att_w_dq"] = w["att_w_dq"][0].astype(BF16)
    wuq = w["att_w_uq"][0].reshape(Q_LORA, MLA_HEADS, QK_DIM)
    half = ROPE_DIM // 2
    zpad = jnp.zeros((Q_LORA, MLA_HEADS, QK_PAD - QK_DIM), F32)
    p["wq1"] = jnp.concatenate([wuq, zpad], axis=-1).reshape(Q_LORA, MLA_HEADS * QK_PAD).astype(BF16)
    swapped = jnp.concatenate([wuq[..., NOPE_DIM + half:], wuq[..., NOPE_DIM:NOPE_DIM + half],
                               jnp.zeros((Q_LORA, MLA_HEADS, LANES - ROPE_DIM), F32)], axis=-1)
    p["wq2"] = swapped.reshape(Q_LORA, MLA_HEADS * LANES).astype(BF16)
    wdkv = w["att_w_dkv"][0]
    zl = jnp.zeros((D_MODEL, LANES - ROPE_DIM), F32)
    p["wdkv"] = jnp.concatenate([wdkv, zl, wdkv[:, KV_LORA + half:], wdkv[:, KV_LORA:KV_LORA + half], zl],
                                axis=-1).astype(BF16)
    wukv = w["att_w_ukv"][0].reshape(KV_LORA, MLA_HEADS, NOPE_DIM + V_DIM)
    p["wk"] = wukv[..., :NOPE_DIM].reshape(KV_LORA, MLA_HEADS * NOPE_DIM).astype(BF16)
    p["wv"] = wukv[..., NOPE_DIM:].reshape(KV_LORA, MLA_HEADS * V_DIM).astype(BF16)
    p["att_w_out"] = w["att_w_out"][0].astype(BF16)
    p["peer_wqt"] = [w["peer_w_q"][l].T.astype(BF16) for l in range(2)]
    p["peer_keys"] = [w["peer_keys"][l].reshape(2 * PEER_HEADS, N_KEYS, HALF_KEY).astype(BF16) for l in range(2)]
    p["peer_u"] = [w["peer_u"][l].astype(BF16) for l in range(2)]
    p["peer_vt"] = [w["peer_v"][l].astype(BF16).reshape(N_EXPERTS // EXPERT_BLOCK, EXPERT_BLOCK, D_MODEL)
                    .transpose(0, 2, 1) for l in range(2)]
    return p


def _rope_tables(pos):
    half = ROPE_DIM // 2
    inv = ROPE_THETA ** (-(jnp.arange(half, dtype=F32) / half))
    ang = pos.astype(F32)[:, None] * inv[None, :]
    cos = jnp.cos(ang)
    sin = jnp.sin(ang)
    z = jnp.zeros((pos.shape[0], LANES - ROPE_DIM), F32)
    return jnp.concatenate([cos, cos, z], axis=-1), jnp.concatenate([-sin, sin, z], axis=-1)


def _peer(hn, x, g2, p, l, final_g=None):
    b, t, d = hn.shape
    e1, e2, th = _peer_route(hn.reshape(b * t, d), p["peer_wqt"][l], p["peer_keys"][l])
    return _peer_experts(hn, p["peer_u"][l], p["peer_vt"][l], e1, e2, th, x, g2, final_g)


def _trunk(x, mods, pos0, conv_prev, h_prev, pool_prev, ckv_prev, kpe_prev, w, p):
    b, t, d = x.shape
    row = lambda a: a.reshape(1, -1)
    sh1, sc1, g1, sh2, sc2, g2 = mods[0]
    z = _inproj(x, row(w["norm1_g"][0]), sh1, sc1, p["rec_w_in"])
    conv0 = jnp.pad(conv_prev, ((0, 0), (8 - (CONV_WIDTH - 1), 0), (0, 0)))
    pool0 = jnp.pad(pool_prev, ((0, 0), (1, 0), (0, 0)))
    mix, h_last = _recmix(z, conv0, h_prev[:, None, :], pool0, w["rec_conv_w"][0], row(w["rec_conv_b"][0]),
                          p["rec_w_a"], row(w["rec_b_a"][0]), p["rec_w_x"], row(w["rec_b_x"][0]),
                          row(w["rec_lambda"][0]), p["pool_w"], row(w["pool_scale"][0]), pos0)
    new_conv = z[:, t - (CONV_WIDTH - 1):, LRU_WIDTH:2 * LRU_WIDTH]
    new_pool = z[:, t - (POOL_MAX - 1):, 2 * LRU_WIDTH:]
    x, hn = _proj_res(mix, p["rec_w_out"], x, g1, row(w["norm2_g"][0]), sh2, sc2)
    x = _peer(hn, x, g2, p, 0)
    sh1, sc1, g1, sh2, sc2, g2 = mods[1]
    pos = pos0 + jnp.arange(t, dtype=jnp.int32)
    cos, sin = _rope_tables(pos)
    q, ckv, kpe = _qkv(x, row(w["norm1_g"][1]), sh1, sc1, p["att_w_dq"], row(w["att_g_q"][0]), p["wq1"], p["wq2"],
                       p["wdkv"], row(w["att_g_kv"][0]), cos, sin)
    ckv_all = jnp.concatenate([ckv_prev, ckv], axis=1)
    kpe_all = jnp.concatenate([kpe_prev, kpe], axis=1)
    kpe128 = jnp.pad(kpe_all, ((0, 0), (0, 0), (0, LANES - ROPE_DIM)))
    k, v = _kvup(ckv_all, kpe128, p["wk"], p["wv"])
    o = _attention(q, k, v)
    x, hn = _proj_res(o, p["att_w_out"], x, g1, row(w["norm2_g"][1]), sh2, sc2)
    y = _peer(hn, x, g2, p, 1, row(w["final_g"]))
    return y, new_conv[None], h_last.reshape(1, b, LRU_WIDTH), new_pool[None], ckv[None], kpe[None]


def kernel(x_prompt, x_sample, c_prompt, c_sample, state_conv, state_lru_h, state_pool, cache_ckv, cache_kpe,
           ada_w, ada_b, norm1_g, norm2_g, rec_w_in, rec_conv_w, rec_conv_b, rec_w_a, rec_b_a, rec_w_x, rec_b_x,
           rec_lambda, pool_w, pool_scale, rec_w_out, att_w_dq, att_g_q, att_w_uq, att_w_dkv, att_g_kv, att_w_ukv,
           att_w_out, peer_w_q, peer_keys, peer_u, peer_v, final_g):
    w = dict(ada_w=ada_w, ada_b=ada_b, norm1_g=norm1_g, norm2_g=norm2_g, rec_w_in=rec_w_in, rec_conv_w=rec_conv_w,
             rec_conv_b=rec_conv_b, rec_w_a=rec_w_a, rec_b_a=rec_b_a, rec_w_x=rec_w_x, rec_b_x=rec_b_x,
             rec_lambda=rec_lambda, pool_w=pool_w, pool_scale=pool_scale, rec_w_out=rec_w_out, att_w_dq=att_w_dq,
             att_g_q=att_g_q, att_w_uq=att_w_uq, att_w_dkv=att_w_dkv, att_g_kv=att_g_kv, att_w_ukv=att_w_ukv,
             att_w_out=att_w_out, peer_w_q=peer_w_q, peer_keys=peer_keys, peer_u=peer_u, peer_v=peer_v,
             final_g=final_g)
    p = _prep_weights(w)
    bp, tp, d = x_prompt.shape
    bs, ts, _ = x_sample.shape
    past = cache_ckv.shape[2]
    depth = ada_w.shape[0]
    rows = bp + bs
    rows_pad = -(-rows // 8) * 8
    c_all = jnp.pad(jnp.concatenate([c_prompt, c_sample], axis=0), ((0, rows_pad - rows), (0, 0)))
    mod = _ada(c_all, ada_w, ada_b)

    def mods_of(lo, n):
        return [[mod[l, lo:lo + n, k * d:(k + 1) * d][:, None, :] for k in range(6)] for l in range(depth)]

    zeros = lambda *s: jnp.zeros(s, F32)
    out_p = _trunk(x_prompt, mods_of(0, bp), 0, zeros(bp, CONV_WIDTH - 1, LRU_WIDTH), zeros(bp, LRU_WIDTH),
                   zeros(bp, POOL_MAX - 1, POOL_WIDTH), zeros(bp, 0, KV_LORA), zeros(bp, 0, ROPE_DIM), w, p)
    out_s = _trunk(x_sample, mods_of(bp, bs), past, state_conv[0], state_lru_h[0], state_pool[0],
                   cache_ckv[0], cache_kpe[0], w, p)
    return (out_p[0], out_s[0], out_p[1], out_s[1], out_p[2], out_s[2],
            out_p[3], out_s[3], out_p[4], out_s[4], out_p[5], out_s[5])
```

```python
import functools
import math

import jax
import jax.numpy as jnp
from jax import lax
from jax.experimental import pallas as pl
from jax.experimental.pallas import tpu as pltpu

F32 = jnp.float32
BF16 = jnp.bfloat16

D_MODEL = 2048
CHUNK = 64
RMS_EPS = 1e-6
LRU_WIDTH = 1024
LRU_HEADS = 8
LRU_HEAD_DIM = 128
CONV_WIDTH = 4
LRU_C = 8.0
POOL_WIDTH = 1024
POOL_WINDOWS = (2, 4, 8, 16)
POOL_GROUP_DIM = 256
POOL_MAX = 16
MIX_IN = 3072
MLA_HEADS = 16
Q_LORA = 512
KV_LORA = 256
NOPE_DIM = 128
ROPE_DIM = 64
V_DIM = 128
QK_DIM = NOPE_DIM + ROPE_DIM
ROPE_THETA = 10000.0
PEER_HEADS = 8
N_KEYS = 128
N_EXPERTS = N_KEYS * N_KEYS
PEER_TOPK = 16
HALF_KEY = 128

LANES = 128
GATE_ROWS = 64
EXPERT_BLOCK = 1024
EXPERT_SUB = 256
N_SUB = EXPERT_BLOCK // EXPERT_SUB
QK_PAD = 256
ATTN_TILE = 512
SOFTMAX_ROWS = 64
Q_SCALE = QK_DIM ** -0.5 * math.log2(math.e)
NEG_BIG = -1e30
VMEM_LIMIT = 56 * 1024 * 1024


def _cparams(sem):
    return pltpu.CompilerParams(dimension_semantics=sem, vmem_limit_bytes=VMEM_LIMIT)


def _const_spec(shape):
    n = len(shape)
    return pl.BlockSpec(shape, lambda *_: (0,) * n)


def _seq_tiles(b, t):
    if t >= 512:
        return 1, 512
    sb = max(1, min(b, 512 // t))
    while b % sb:
        sb -= 1
    return sb, t


def _rmsnorm(x, g):
    return x * lax.rsqrt(jnp.mean(x * x, axis=-1, keepdims=True) + RMS_EPS) * g


def _norm_mod(x, g, sh, sc):
    return _rmsnorm(x, g) * (1.0 + sc) + sh


def _ada_kernel(c_ref, w_ref, b_ref, o_ref):
    c = c_ref[...]
    sc = (c * jax.nn.sigmoid(c)).astype(BF16)
    o_ref[0] = jnp.dot(sc, w_ref[0].astype(BF16), preferred_element_type=F32) + b_ref[0]


def _ada(c_all, ada_w, ada_b):
    depth, d, n = ada_w.shape
    rows = c_all.shape[0]
    tn = 1024
    return pl.pallas_call(
        _ada_kernel,
        grid=(depth, n // tn),
        in_specs=[pl.BlockSpec((rows, d), lambda l, j: (0, 0)),
                  pl.BlockSpec((1, d, tn), lambda l, j: (l, 0, j)),
                  pl.BlockSpec((1, 1, tn), lambda l, j: (l, 0, j))],
        out_specs=pl.BlockSpec((1, rows, tn), lambda l, j: (l, 0, j)),
        out_shape=jax.ShapeDtypeStruct((depth, rows, n), F32),
        compiler_params=_cparams(("arbitrary", "arbitrary")),
        name="ada_mod",
    )(c_all, ada_w, ada_b.reshape(depth, 1, n))


def _inproj_kernel(x_ref, g_ref, sh_ref, sc_ref, w_ref, o_ref):
    sb, tt, d = x_ref.shape
    hn = _norm_mod(x_ref[...], g_ref[...], sh_ref[...], sc_ref[...])
    z = jnp.dot(hn.reshape(sb * tt, d).astype(BF16), w_ref[...], preferred_element_type=F32)
    o_ref[...] = z.reshape(sb, tt, -1)


def _inproj(x, g, sh, sc, w):
    b, t, d = x.shape
    n = w.shape[1]
    sb, tt = _seq_tiles(b, t)
    tt = min(tt, 256)
    seq = pl.BlockSpec((sb, 1, d), lambda i, j: (i, 0, 0))
    return pl.pallas_call(
        _inproj_kernel,
        grid=(b // sb, t // tt),
        in_specs=[pl.BlockSpec((sb, tt, d), lambda i, j: (i, j, 0)),
                  _const_spec((1, d)), seq, seq, _const_spec((d, n))],
        out_specs=pl.BlockSpec((sb, tt, n), lambda i, j: (i, j, 0)),
        out_shape=jax.ShapeDtypeStruct((b, t, n), F32),
        compiler_params=_cparams(("arbitrary", "arbitrary")),
        name="l0_inproj",
    )(x, g, sh, sc, w)


def _shift_rows(x, d, fill):
    rolled = pltpu.roll(x, d, 0)
    rows = lax.broadcasted_iota(jnp.int32, x.shape, 0)
    return jnp.where(rows >= d, rolled, fill)


def _gelu_tanh(x):
    return 0.5 * x * (1.0 + jnp.tanh(math.sqrt(2.0 / math.pi) * (x + 0.044715 * (x * x * x))))


def _recmix_kernel(z_ref, conv0_ref, h0_ref, pool0_ref, cw_ref, cb_ref, wa_ref, ba_ref, wx_ref, bx_ref,
                   lam_ref, pw_ref, ps_ref, mix_ref, hlast_ref, ctail, ptail, hstate, ext, *, pos0):
    sb, tt, _ = z_ref.shape
    j = pl.program_id(1)

    @pl.when(j == 0)
    def _():
        ctail[...] = conv0_ref[...]
        ptail[...] = pool0_ref[...]
        hstate[...] = h0_ref[...]

    lam = lam_ref[...]
    neg = -lam
    softplus_neg_lam = jnp.maximum(neg, 0.0) + jnp.log1p(jnp.exp(-jnp.abs(neg)))
    pos = pos0 + j * tt + lax.broadcasted_iota(jnp.int32, (tt, 1), 0)

    def per_seq(s, carry):
        rec = z_ref[s, :, LRU_WIDTH:2 * LRU_WIDTH]
        ext[0:8, :] = ctail[s]
        ext[8:8 + tt, :] = rec
        xc = cb_ref[...] + sum(ext[5 + k:5 + k + tt, :] * cw_ref[k:k + 1, :] for k in range(CONV_WIDTH))
        ctail[s] = ext[tt:tt + 8, :]
        xcb = xc.astype(BF16)
        r_parts, i_parts = [], []
        for h in range(LRU_HEADS):
            sl = slice(h * LRU_HEAD_DIM, (h + 1) * LRU_HEAD_DIM)
            r_parts.append(jnp.dot(xcb[:, sl], wa_ref[h], preferred_element_type=F32))
            i_parts.append(jnp.dot(xcb[:, sl], wx_ref[h], preferred_element_type=F32))
        r = jax.nn.sigmoid(jnp.concatenate(r_parts, axis=-1) + ba_ref[...])
        i = jax.nn.sigmoid(jnp.concatenate(i_parts, axis=-1) + bx_ref[...])
        log_a = -LRU_C * r * softplus_neg_lam
        a = jnp.exp(log_a)
        th = jnp.tanh(log_a)
        mult = jnp.sqrt(-2.0 * th / (1.0 - th))
        mult = jnp.where(pos == 0, 1.0, mult)
        u = mult * (i * xc)
        acc_a, acc_b = a, u
        d = 1
        while d < tt:
            a_sh = _shift_rows(acc_a, d, 1.0)
            b_sh = _shift_rows(acc_b, d, 0.0)
            acc_b = acc_a * b_sh + acc_b
            acc_a = acc_a * a_sh
            d *= 2
        hseq = acc_a * hstate[s] + acc_b
        hstate[s] = hseq[tt - 1:tt, :]
        gate = z_ref[s, :, 0:LRU_WIDTH]
        mix_ref[s, :, 0:LRU_WIDTH] = (_gelu_tanh(gate) * hseq).astype(mix_ref.dtype)
        pin = z_ref[s, :, 2 * LRU_WIDTH:]
        ext[0:16, :] = ptail[s]
        ext[16:16 + tt, :] = pin
        ptail[s] = ext[tt:tt + 16, :]
        for g, w in enumerate(POOL_WINDOWS):
            sl = slice(g * POOL_GROUP_DIM, (g + 1) * POOL_GROUP_DIM)
            win = sum(ext[16 - k:16 - k + tt, sl] for k in range(w))
            cnt = jnp.minimum(pos + 1, w).astype(F32)
            pooled = win / cnt - pin[:, sl]
            bo = jnp.dot(pooled.astype(BF16), pw_ref[g], preferred_element_type=F32) * ps_ref[:, sl]
            mix_ref[s, :, LRU_WIDTH + g * POOL_GROUP_DIM:LRU_WIDTH + (g + 1) * POOL_GROUP_DIM] = bo.astype(mix_ref.dtype)
        return carry

    lax.fori_loop(0, sb, per_seq, 0)

    @pl.when(j == pl.num_programs(1) - 1)
    def _():
        hlast_ref[...] = hstate[...]


def _recmix(z, conv0, h0, pool0, cw, cb, wa, ba, wx, bx, lam, pw, ps, pos0):
    b, t, _ = z.shape
    sb, tt = _seq_tiles(b, t)
    tt = min(tt, 256)
    c = LRU_WIDTH
    row = _const_spec((1, c))
    return pl.pallas_call(
        functools.partial(_recmix_kernel, pos0=pos0),
        grid=(b // sb, t // tt),
        in_specs=[pl.BlockSpec((sb, tt, MIX_IN), lambda i, j: (i, j, 0)),
                  pl.BlockSpec((sb, 8, c), lambda i, j: (i, 0, 0)),
                  pl.BlockSpec((sb, 1, c), lambda i, j: (i, 0, 0)),
                  pl.BlockSpec((sb, 16, c), lambda i, j: (i, 0, 0)),
                  _const_spec((CONV_WIDTH, c)), row,
                  _const_spec((LRU_HEADS, LRU_HEAD_DIM, LRU_HEAD_DIM)), row,
                  _const_spec((LRU_HEADS, LRU_HEAD_DIM, LRU_HEAD_DIM)), row, row,
                  _const_spec((len(POOL_WINDOWS), POOL_GROUP_DIM, POOL_GROUP_DIM)), row],
        out_specs=[pl.BlockSpec((sb, tt, 2 * c), lambda i, j: (i, j, 0)),
                   pl.BlockSpec((sb, 1, c), lambda i, j: (i, 0, 0))],
        out_shape=[jax.ShapeDtypeStruct((b, t, 2 * c), BF16),
                   jax.ShapeDtypeStruct((b, 1, c), F32)],
        scratch_shapes=[pltpu.VMEM((sb, 8, c), F32), pltpu.VMEM((sb, 16, c), F32),
                        pltpu.VMEM((sb, 1, c), F32), pltpu.VMEM((tt + 16, c), F32)],
        compiler_params=_cparams(("arbitrary", "arbitrary")),
        name="l0_recmix",
    )(z, conv0, h0, pool0, cw, cb, wa, ba, wx, bx, lam, pw, ps)


def _proj_res_kernel(a_ref, w_ref, x_ref, g1_ref, n2_ref, sh_ref, sc_ref, xo_ref, ho_ref):
    sb, tt, k = a_ref.shape
    mix = jnp.dot(a_ref[...].reshape(sb * tt, k), w_ref[...], preferred_element_type=F32)
    xn = x_ref[...] + g1_ref[...] * mix.reshape(sb, tt, -1)
    xo_ref[...] = xn
    ho_ref[...] = _norm_mod(xn, n2_ref[...], sh_ref[...], sc_ref[...]).astype(ho_ref.dtype)


def _proj_res(a, w, x, g1, n2, sh2, sc2):
    b, t, d = x.shape
    k = a.shape[-1]
    sb, tt = _seq_tiles(b, t)
    seq = pl.BlockSpec((sb, 1, d), lambda i, j: (i, 0, 0))
    tile = pl.BlockSpec((sb, tt, d), lambda i, j: (i, j, 0))
    return pl.pallas_call(
        _proj_res_kernel,
        grid=(b // sb, t // tt),
        in_specs=[pl.BlockSpec((sb, tt, k), lambda i, j: (i, j, 0)), _const_spec((k, d)), tile,
                  seq, _const_spec((1, d)), seq, seq],
        out_specs=[tile, tile],
        out_shape=[jax.ShapeDtypeStruct((b, t, d), F32), jax.ShapeDtypeStruct((b, t, d), BF16)],
        compiler_params=_cparams(("arbitrary", "arbitrary")),
        name="proj_res_norm",
    )(a, w, x, g1, n2, sh2, sc2)


def _top16_rows(vals):
    tm = vals.shape[1]
    rows16 = lax.broadcasted_iota(jnp.int32, (PEER_TOPK, tm), 0)
    top = jnp.zeros((PEER_TOPK, tm), F32)
    for k in range(PEER_TOPK):
        mx = jnp.max(vals, axis=0, keepdims=True)
        top = jnp.where(rows16 == k, mx, top)
        vals = jnp.where(vals == mx, -jnp.inf, vals)
    return top


def _rank_pairs(t1, t2, op, fill):
    tm = t1.shape[1]
    row = lax.broadcasted_iota(jnp.int32, (PEER_TOPK, tm), 0)

    def put(slab, val, r0, n):
        return jnp.where(row < r0, slab, jnp.where(row < r0 + n, val, slab))

    def a_with_prefix(slab, a, r0, n):
        t2s = pltpu.roll(t2, r0, 0) if r0 else t2
        return put(slab, op(t1[a:a + 1, :], t2s), r0, n)

    blank = jnp.full((PEER_TOPK, tm), fill, F32)
    slab0 = op(t1[0:1, :], t2)
    slab1 = a_with_prefix(a_with_prefix(a_with_prefix(blank, 1, 0, 8), 2, 8, 5), 4, 13, 3)
    slab2 = blank
    for a, r0, n in ((3, 0, 4), (5, 4, 2), (6, 6, 2), (7, 8, 2)):
        slab2 = a_with_prefix(slab2, a, r0, n)
    tail = op(pltpu.roll(t1, 2, 0), t2[0:1, :])
    slab2 = put(slab2, tail, 10, 6)
    slab3 = put(blank, tail, 0, 2)
    return jnp.concatenate([slab0, slab1, slab2, slab3], axis=0)


def _peer_route_kernel(x_ref, wqt_ref, keys_ref, e1_ref, e2_ref, th_ref, qt_ref):
    qt_ref[...] = lax.dot_general(wqt_ref[...], x_ref[...], (((1,), (1,)), ((), ())),
                                  preferred_element_type=F32).astype(BF16)

    def per_head(h):
        r1 = pl.multiple_of(h * 2 * HALF_KEY, 2 * HALF_KEY)
        r2 = pl.multiple_of(h * 2 * HALF_KEY + HALF_KEY, HALF_KEY)
        s1 = jnp.dot(keys_ref[2 * h], qt_ref[pl.ds(r1, HALF_KEY), :], preferred_element_type=F32)
        s2 = jnp.dot(keys_ref[2 * h + 1], qt_ref[pl.ds(r2, HALF_KEY), :], preferred_element_type=F32)
        t1 = _top16_rows(s1)
        t2 = _top16_rows(s2)
        cand = _rank_pairs(t1, t2, jnp.add, -jnp.inf)
        vals = cand
        tau = None
        for _ in range(PEER_TOPK):
            tau = jnp.max(vals, axis=0, keepdims=True)
            vals = jnp.where(vals == tau, -jnp.inf, vals)
        sel = cand >= tau
        m1 = t1[0:1, :]
        m2 = t2[0:1, :]
        z = jnp.sum(jnp.where(sel, jnp.exp(cand - (m1 + m2)), 0.0), axis=0, keepdims=True)
        rz = 1.0 / z
        e1_ref[h] = jnp.exp(s1 - m1) * rz
        e2_ref[h] = jnp.exp(s2 - m2)
        prod = _rank_pairs(jnp.exp(t1 - m1) * rz, jnp.exp(t2 - m2), jnp.multiply, jnp.inf)
        th_ref[h] = jnp.min(jnp.where(sel, prod, jnp.inf), axis=0, keepdims=True)

    def head_pair(hp, carry):
        per_head(2 * hp)
        per_head(2 * hp + 1)
        return carry

    lax.fori_loop(0, PEER_HEADS // 2, head_pair, 0)


def _peer_route(hn, wqt, keys):
    m, d = hn.shape
    tm = 256
    return pl.pallas_call(
        _peer_route_kernel,
        grid=(m // tm,),
        in_specs=[pl.BlockSpec((tm, d), lambda i: (i, 0)), _const_spec((d, d)),
                  _const_spec((2 * PEER_HEADS, N_KEYS, HALF_KEY))],
        out_specs=[pl.BlockSpec((PEER_HEADS, N_KEYS, tm), lambda i: (0, 0, i)),
                   pl.BlockSpec((PEER_HEADS, N_KEYS, tm), lambda i: (0, 0, i)),
                   pl.BlockSpec((PEER_HEADS, 1, tm), lambda i: (0, 0, i))],
        out_shape=[jax.ShapeDtypeStruct((PEER_HEADS, N_KEYS, m), F32),
                   jax.ShapeDtypeStruct((PEER_HEADS, N_KEYS, m), F32),
                   jax.ShapeDtypeStruct((PEER_HEADS, 1, m), F32)],
        scratch_shapes=[pltpu.VMEM((d, tm), BF16)],
        compiler_params=_cparams(("arbitrary",)),
        name="peer_route",
    )(hn, wqt, keys)


def _gelu_erf(x):
    return 0.5 * x * (1.0 + lax.erf(x * math.sqrt(0.5)))


def _peer_expert_kernel(h_ref, *rest, final_norm):
    u_refs, vt_refs = rest[:N_SUB], rest[N_SUB:2 * N_SUB]
    e1_ref, e2_ref, th_ref, x_ref, g2_ref = rest[2 * N_SUB:2 * N_SUB + 5]
    tail = rest[2 * N_SUB + 5:]
    if final_norm:
        fg_ref, o_ref, acc_ref, s_ref, a_ref = tail
    else:
        o_ref, acc_ref, s_ref, a_ref = tail
    e = pl.program_id(2)
    eb, tm = s_ref.shape
    sb_, tt, d = h_ref.shape
    hb = h_ref[...].reshape(tm, d)

    @pl.when(e == 0)
    def _():
        acc_ref[...] = jnp.zeros_like(acc_ref)

    for sb in range(N_SUB):
        rows = slice(sb * EXPERT_SUB, (sb + 1) * EXPERT_SUB)
        s_ref[rows, :] = lax.dot_general(u_refs[sb][...], hb, (((1,), (1,)), ((), ())),
                                         preferred_element_type=F32)
    for ii in range(eb // N_KEYS):
        for jh in range(N_KEYS // GATE_ROWS):
            jrows = slice(jh * GATE_ROWS, (jh + 1) * GATE_ROWS)
            rows = slice(ii * N_KEYS + jh * GATE_ROWS, ii * N_KEYS + (jh + 1) * GATE_ROWS)
            for lt in range(tm // LANES):
                cols = slice(lt * LANES, (lt + 1) * LANES)
                g = jnp.zeros((GATE_ROWS, LANES), F32)
                for h in range(PEER_HEADS):
                    p = e2_ref[h, jrows, cols] * e1_ref[h, ii:ii + 1, cols]
                    g = g + jnp.where(p >= th_ref[h, :, cols], p, 0.0)
                a_ref[rows, cols] = (_gelu_erf(s_ref[rows, cols]) * g).astype(a_ref.dtype)
    out = acc_ref[...]
    for sb in range(N_SUB):
        out = out + jnp.dot(vt_refs[sb][0], a_ref[sb * EXPERT_SUB:(sb + 1) * EXPERT_SUB, :],
                            preferred_element_type=F32)
    acc_ref[...] = out

    @pl.when(e == pl.num_programs(2) - 1)
    def _():
        y = x_ref[...] + g2_ref[...] * acc_ref[...].T.reshape(sb_, tt, d)
        o_ref[...] = _rmsnorm(y, fg_ref[...]) if final_norm else y


def _peer_experts(hn, u, vt, e1, e2, th, x, g2, final_g):
    b, t, d = hn.shape
    sb, tt = _seq_tiles(b, t)
    tm = sb * tt
    nt = t // tt
    eb = EXPERT_BLOCK
    tile = pl.BlockSpec((sb, tt, d), lambda i, j, e: (i, j, 0))
    sub = lambda k: (lambda i, j, e: (e * N_SUB + k, 0))
    sub3 = lambda k: (lambda i, j, e: (e * N_SUB + k, 0, 0))
    in_specs = ([tile]
                + [pl.BlockSpec((EXPERT_SUB, d), sub(k)) for k in range(N_SUB)]
                + [pl.BlockSpec((1, d, EXPERT_SUB), sub3(k)) for k in range(N_SUB)]
                + [pl.BlockSpec((PEER_HEADS, eb // N_KEYS, tm), lambda i, j, e: (0, e, i * nt + j)),
                   pl.BlockSpec((PEER_HEADS, N_KEYS, tm), lambda i, j, e: (0, 0, i * nt + j)),
                   pl.BlockSpec((PEER_HEADS, 1, tm), lambda i, j, e: (0, 0, i * nt + j)),
                   tile,
                   pl.BlockSpec((sb, 1, d), lambda i, j, e: (i, 0, 0))])
    args = [hn] + [u] * N_SUB + [vt] * N_SUB + [e1, e2, th, x, g2]
    if final_g is not None:
        in_specs.append(_const_spec((1, d)))
        args.append(final_g)
    return pl.pallas_call(
        functools.partial(_peer_expert_kernel, final_norm=final_g is not None),
        grid=(b // sb, nt, N_EXPERTS // eb),
        in_specs=in_specs,
        out_specs=tile,
        out_shape=jax.ShapeDtypeStruct((b, t, d), F32),
        scratch_shapes=[pltpu.VMEM((d, tm), F32), pltpu.VMEM((eb, tm), F32), pltpu.VMEM((eb, tm), BF16)],
        compiler_params=_cparams(("arbitrary", "arbitrary", "arbitrary")),
        name="peer_experts",
    )(*args)


def _qkv_kernel(x_ref, g_ref, sh_ref, sc_ref, wdq_ref, gq_ref, wq1_ref, wq2_ref, wdkv_ref, gkv_ref,
                cos_ref, sin_ref, q_ref, ckv_ref, kpe_ref):
    sb, tt, d = x_ref.shape
    hn = _norm_mod(x_ref[...], g_ref[...], sh_ref[...], sc_ref[...]).reshape(sb * tt, d).astype(BF16)
    cq = (_rmsnorm(jnp.dot(hn, wdq_ref[...], preferred_element_type=F32), gq_ref[...]) * Q_SCALE).astype(BF16)
    q1 = jnp.dot(cq, wq1_ref[...], preferred_element_type=F32)
    q2 = jnp.dot(cq, wq2_ref[...], preferred_element_type=F32)
    cos = cos_ref[...][None]
    sin = sin_ref[...][None]
    for h in range(MLA_HEADS):
        q_ref[:, :, h * QK_PAD:h * QK_PAD + NOPE_DIM] = (
            q1[:, h * QK_PAD:h * QK_PAD + NOPE_DIM].reshape(sb, tt, NOPE_DIM).astype(q_ref.dtype))
        pe = (q1[:, h * QK_PAD + NOPE_DIM:(h + 1) * QK_PAD].reshape(sb, tt, LANES) * cos
              + q2[:, h * LANES:(h + 1) * LANES].reshape(sb, tt, LANES) * sin)
        q_ref[:, :, h * QK_PAD + NOPE_DIM:(h + 1) * QK_PAD] = pe.astype(q_ref.dtype)
    kv = jnp.dot(hn, wdkv_ref[...], preferred_element_type=F32)
    ckv_ref[...] = _rmsnorm(kv[:, :KV_LORA], gkv_ref[...]).reshape(sb, tt, KV_LORA)
    kpe = (kv[:, KV_LORA:KV_LORA + LANES].reshape(sb, tt, LANES) * cos
           + kv[:, KV_LORA + LANES:].reshape(sb, tt, LANES) * sin)
    kpe_ref[...] = kpe[:, :, :ROPE_DIM]


def _qkv(x, g, sh, sc, wdq, gq, wq1, wq2, wdkv, gkv, cos, sin):
    b, t, d = x.shape
    sb, tt = _seq_tiles(b, t)
    tt = min(tt, 256)
    seq = pl.BlockSpec((sb, 1, d), lambda i, j: (i, 0, 0))
    rope = pl.BlockSpec((tt, LANES), lambda i, j: (j, 0))
    return pl.pallas_call(
        _qkv_kernel,
        grid=(b // sb, t // tt),
        in_specs=[pl.BlockSpec((sb, tt, d), lambda i, j: (i, j, 0)), _const_spec((1, d)), seq, seq,
                  _const_spec(wdq.shape), _const_spec((1, Q_LORA)), _const_spec(wq1.shape), _const_spec(wq2.shape),
                  _const_spec(wdkv.shape), _const_spec((1, KV_LORA)), rope, rope],
        out_specs=[pl.BlockSpec((sb, tt, MLA_HEADS * QK_PAD), lambda i, j: (i, j, 0)),
                   pl.BlockSpec((sb, tt, KV_LORA), lambda i, j: (i, j, 0)),
                   pl.BlockSpec((sb, tt, ROPE_DIM), lambda i, j: (i, j, 0))],
        out_shape=[jax.ShapeDtypeStruct((b, t, MLA_HEADS * QK_PAD), BF16),
                   jax.ShapeDtypeStruct((b, t, KV_LORA), F32),
                   jax.ShapeDtypeStruct((b, t, ROPE_DIM), F32)],
        compiler_params=_cparams(("arbitrary", "arbitrary")),
        name="l1_qkv",
    )(x, g, sh, sc, wdq, gq, wq1, wq2, wdkv, gkv, cos, sin)


def _kvup_kernel(ckv_ref, kpe_ref, wk_ref, wv_ref, k_ref, v_ref):
    c = ckv_ref[0].astype(BF16)
    kn = jnp.dot(c, wk_ref[...], preferred_element_type=F32)
    v_ref[0] = jnp.dot(c, wv_ref[...], preferred_element_type=F32).astype(v_ref.dtype)
    kpe = kpe_ref[0].astype(k_ref.dtype)
    for h in range(MLA_HEADS):
        k_ref[0, :, h * QK_PAD:h * QK_PAD + NOPE_DIM] = kn[:, h * NOPE_DIM:(h + 1) * NOPE_DIM].astype(k_ref.dtype)
        k_ref[0, :, h * QK_PAD + NOPE_DIM:(h + 1) * QK_PAD] = kpe


def _kvup(ckv, kpe128, wk, wv):
    b, t, _ = ckv.shape
    tt = max(c for c in range(16, 513, 16) if t % c == 0)
    return pl.pallas_call(
        _kvup_kernel,
        grid=(b, t // tt),
        in_specs=[pl.BlockSpec((1, tt, KV_LORA), lambda i, j: (i, j, 0)),
                  pl.BlockSpec((1, tt, LANES), lambda i, j: (i, j, 0)),
                  _const_spec(wk.shape), _const_spec(wv.shape)],
        out_specs=[pl.BlockSpec((1, tt, MLA_HEADS * QK_PAD), lambda i, j: (i, j, 0)),
                   pl.BlockSpec((1, tt, MLA_HEADS * V_DIM), lambda i, j: (i, j, 0))],
        out_shape=[jax.ShapeDtypeStruct((b, t, MLA_HEADS * QK_PAD), BF16),
                   jax.ShapeDtypeStruct((b, t, MLA_HEADS * V_DIM), BF16)],
        compiler_params=_cparams(("arbitrary", "arbitrary")),
        name="l1_kvup",
    )(ckv, kpe128, wk, wv)


def _qk(q, k):
    return lax.dot_general(q, k, (((1,), (1,)), ((), ())), preferred_element_type=F32)


def _attn_stream_kernel(q_ref, k_ref, v_ref, bias_ref, o_ref, s0, s1, p0, p1, a0, a1, m_run, l_run, acc):
    t = q_ref.shape[1]
    qi = pl.program_id(2)
    q = q_ref[0]

    def scores(k):
        start = pl.multiple_of(jnp.minimum(k, qi) * t, t)
        which = jnp.where(k == qi, 1, jnp.where(k > qi, 2, 0))
        return _qk(q, k_ref[0, pl.ds(start, t), :]) + bias_ref[which]

    def values(k):
        start = pl.multiple_of(jnp.clip(k, 0, qi) * t, t)
        return v_ref[0, pl.ds(start, t), :]

    def half(k, s_cur, s_next, p_cur, p_prev, a_cur, a_prev):
        s_next[...] = scores(k + 1)
        acc[...] = a_prev[...] * acc[...] + jnp.dot(p_prev[...], values(k - 1), preferred_element_type=F32)
        m_old = m_run[...]
        m_new = jnp.maximum(m_old, jnp.max(s_cur[...], axis=-1, keepdims=True))
        alpha = jnp.exp2(m_old - m_new)
        a_cur[...] = alpha
        m_run[...] = m_new
        p = jnp.exp2(s_cur[...] - jnp.concatenate([m_new] * (t // LANES), axis=1))
        l_run[...] = alpha * l_run[...] + jnp.sum(p, axis=-1, keepdims=True)
        p_cur[...] = p.astype(p_cur.dtype)

    s0[...] = scores(0)
    p1[...] = jnp.zeros(p1.shape, p1.dtype)
    a1[...] = jnp.ones(a1.shape, F32)
    acc[...] = jnp.zeros(acc.shape, F32)
    m_run[...] = jnp.full(m_run.shape, NEG_BIG, F32)
    l_run[...] = jnp.zeros(l_run.shape, F32)

    def body(j, carry):
        half(2 * j, s0, s1, p0, p1, a0, a1)
        half(2 * j + 1, s1, s0, p1, p0, a1, a0)
        return carry

    pairs = (qi + 2) // 2
    lax.fori_loop(0, pairs, body, 0)
    out = a1[...] * acc[...] + jnp.dot(p1[...], values(2 * pairs - 1), preferred_element_type=F32)
    o_ref[0] = (out / l_run[...]).astype(o_ref.dtype)


def _attn_single_kernel(q_ref, k_ref, v_ref, o_ref, *, q_off):
    tq = q_ref.shape[1]
    tk = k_ref.shape[1]
    s = _qk(q_ref[0], k_ref[0])
    qc = (q_off + lax.broadcasted_iota(jnp.int32, (tq, tk), 0)) // CHUNK
    kc = lax.broadcasted_iota(jnp.int32, (tq, tk), 1) // CHUNK
    s = jnp.where(kc <= qc, s, NEG_BIG)
    p = jnp.exp2(s - jnp.max(s, axis=-1, keepdims=True))
    l = jnp.sum(p, axis=-1, keepdims=True)
    o_ref[0] = (jnp.dot(p.astype(BF16), v_ref[0], preferred_element_type=F32) / l).astype(o_ref.dtype)


def _attention(q, k, v):
    b, tq_all, _ = q.shape
    tk_all = k.shape[1]
    out_shape = jax.ShapeDtypeStruct((b, tq_all, MLA_HEADS * V_DIM), BF16)
    kv_specs = [pl.BlockSpec((1, tk_all, QK_PAD), lambda i, h, j: (i, 0, h)),
                pl.BlockSpec((1, tk_all, V_DIM), lambda i, h, j: (i, 0, h))]
    if tq_all == tk_all and tq_all % ATTN_TILE == 0:
        t = ATTN_TILE
        r = lax.broadcasted_iota(jnp.int32, (t, t), 0) // CHUNK
        c = lax.broadcasted_iota(jnp.int32, (t, t), 1) // CHUNK
        bias = jnp.stack([jnp.zeros((t, t), F32), jnp.where(c <= r, 0.0, NEG_BIG).astype(F32),
                          jnp.full((t, t), NEG_BIG, F32)])
        return pl.pallas_call(
            _attn_stream_kernel,
            grid=(b, MLA_HEADS, tq_all // t),
            in_specs=[pl.BlockSpec((1, t, QK_PAD), lambda i, h, j: (i, j, h))] + kv_specs + [_const_spec((3, t, t))],
            out_specs=pl.BlockSpec((1, t, V_DIM), lambda i, h, j: (i, j, h)),
            out_shape=out_shape,
            scratch_shapes=[pltpu.VMEM((t, t), F32), pltpu.VMEM((t, t), F32),
                            pltpu.VMEM((t, t), BF16), pltpu.VMEM((t, t), BF16),
                            pltpu.VMEM((t, LANES), F32), pltpu.VMEM((t, LANES), F32),
                            pltpu.VMEM((t, LANES), F32), pltpu.VMEM((t, LANES), F32), pltpu.VMEM((t, V_DIM), F32)],
            compiler_params=_cparams(("arbitrary", "arbitrary", "arbitrary")),
            name="l1_attention",
        )(q, k, v, bias)
    assert tq_all <= ATTN_TILE, "history + long query block is not a shape this trunk step has"
    return pl.pallas_call(
        functools.partial(_attn_single_kernel, q_off=tk_all - tq_all),
        grid=(b, MLA_HEADS, 1),
        in_specs=[pl.BlockSpec((1, tq_all, QK_PAD), lambda i, h, j: (i, 0, h))] + kv_specs,
        out_specs=pl.BlockSpec((1, tq_all, V_DIM), lambda i, h, j: (i, 0, h)),
        out_shape=out_shape,
        compiler_params=_cparams(("arbitrary", "arbitrary", "arbitrary")),
        name="l1_attention_step",
    )(q, k, v)


def _prep_weights(w):
    p = {}
    p["rec_w_in"] = w["rec_w_in"][0].astype(BF16)
    p["rec_w_a"] = w["rec_w_a"][0].astype(BF16)
    p["rec_w_x"] = w["rec_w_x"][0].astype(BF16)
    p["pool_w"] = w["pool_w"][0].astype(BF16)
    p["rec_w_out"] = w["rec_w_out"][0].astype(BF16)
    p["att_w_dq"] = w["att_w_dq"][0].astype(BF16)
    wuq = w["att_w_uq"][0].reshape(Q_LORA, MLA_HEADS, QK_DIM)
    half = ROPE_DIM // 2
    zpad = jnp.zeros((Q_LORA, MLA_HEADS, QK_PAD - QK_DIM), F32)
    p["wq1"] = jnp.concatenate([wuq, zpad], axis=-1).reshape(Q_LORA, MLA_HEADS * QK_PAD).astype(BF16)
    swapped = jnp.concatenate([wuq[..., NOPE_DIM + half:], wuq[..., NOPE_DIM:NOPE_DIM + half],
                               jnp.zeros((Q_LORA, MLA_HEADS, LANES - ROPE_DIM), F32)], axis=-1)
    p["wq2"] = swapped.reshape(Q_LORA, MLA_HEADS * LANES).astype(BF16)
    wdkv = w["att_w_dkv"][0]
    zl = jnp.zeros((D_MODEL, LANES - ROPE_DIM), F32)
    p["wdkv"] = jnp.concatenate([wdkv, zl, wdkv[:, KV_LORA + half:], wdkv[:, KV_LORA:KV_LORA + half], zl],
                                axis=-1).astype(BF16)
    wukv = w["att_w_ukv"][0].reshape(KV_LORA, MLA_HEADS, NOPE_DIM + V_DIM)
    p["wk"] = wukv[..., :NOPE_DIM].reshape(KV_LORA, MLA_HEADS * NOPE_DIM).astype(BF16)
    p["wv"] = wukv[..., NOPE_DIM:].reshape(KV_LORA, MLA_HEADS * V_DIM).astype(BF16)
    p["att_w_out"] = w["att_w_out"][0].astype(BF16)
    p["peer_wqt"] = [w["peer_w_q"][l].T.astype(BF16) for l in range(2)]
    p["peer_keys"] = [w["peer_keys"][l].reshape(2 * PEER_HEADS, N_KEYS, HALF_KEY).astype(BF16) for l in range(2)]
    p["peer_u"] = [w["peer_u"][l].astype(BF16) for l in range(2)]
    p["peer_vt"] = [w["peer_v"][l].astype(BF16).reshape(N_EXPERTS // EXPERT_SUB, EXPERT_SUB, D_MODEL)
                    .transpose(0, 2, 1) for l in range(2)]
    return p


def _rope_tables(pos):
    half = ROPE_DIM // 2
    inv = ROPE_THETA ** (-(jnp.arange(half, dtype=F32) / half))
    ang = pos.astype(F32)[:, None] * inv[None, :]
    cos = jnp.cos(ang)
    sin = jnp.sin(ang)
    z = jnp.zeros((pos.shape[0], LANES - ROPE_DIM), F32)
    return jnp.concatenate([cos, cos, z], axis=-1), jnp.concatenate([-sin, sin, z], axis=-1)


def _peer(hn, x, g2, p, l, final_g=None):
    b, t, d = hn.shape
    e1, e2, th = _peer_route(hn.reshape(b * t, d), p["peer_wqt"][l], p["peer_keys"][l])
    return _peer_experts(hn, p["peer_u"][l], p["peer_vt"][l], e1, e2, th, x, g2, final_g)


def _trunk(x, mods, pos0, conv_prev, h_prev, pool_prev, ckv_prev, kpe_prev, w, p):
    b, t, d = x.shape
    row = lambda a: a.reshape(1, -1)
    sh1, sc1, g1, sh2, sc2, g2 = mods[0]
    z = _inproj(x, row(w["norm1_g"][0]), sh1, sc1, p["rec_w_in"])
    conv0 = jnp.pad(conv_prev, ((0, 0), (8 - (CONV_WIDTH - 1), 0), (0, 0)))
    pool0 = jnp.pad(pool_prev, ((0, 0), (1, 0), (0, 0)))
    mix, h_last = _recmix(z, conv0, h_prev[:, None, :], pool0, w["rec_conv_w"][0], row(w["rec_conv_b"][0]),
                          p["rec_w_a"], row(w["rec_b_a"][0]), p["rec_w_x"], row(w["rec_b_x"][0]),
                          row(w["rec_lambda"][0]), p["pool_w"], row(w["pool_scale"][0]), pos0)
    new_conv = z[:, t - (CONV_WIDTH - 1):, LRU_WIDTH:2 * LRU_WIDTH]
    new_pool = z[:, t - (POOL_MAX - 1):, 2 * LRU_WIDTH:]
    x, hn = _proj_res(mix, p["rec_w_out"], x, g1, row(w["norm2_g"][0]), sh2, sc2)
    x = _peer(hn, x, g2, p, 0)
    sh1, sc1, g1, sh2, sc2, g2 = mods[1]
    pos = pos0 + jnp.arange(t, dtype=jnp.int32)
    cos, sin = _rope_tables(pos)
    q, ckv, kpe = _qkv(x, row(w["norm1_g"][1]), sh1, sc1, p["att_w_dq"], row(w["att_g_q"][0]), p["wq1"], p["wq2"],
                       p["wdkv"], row(w["att_g_kv"][0]), cos, sin)
    ckv_all = jnp.concatenate([ckv_prev, ckv], axis=1)
    kpe_all = jnp.concatenate([kpe_prev, kpe], axis=1)
    kpe128 = jnp.pad(kpe_all, ((0, 0), (0, 0), (0, LANES - ROPE_DIM)))
    k, v = _kvup(ckv_all, kpe128, p["wk"], p["wv"])
    o = _attention(q, k, v)
    x, hn = _proj_res(o, p["att_w_out"], x, g1, row(w["norm2_g"][1]), sh2, sc2)
    y = _peer(hn, x, g2, p, 1, row(w["final_g"]))
    return y, new_conv[None], h_last.reshape(1, b, LRU_WIDTH), new_pool[None], ckv[None], kpe[None]


def kernel(x_prompt, x_sample, c_prompt, c_sample, state_conv, state_lru_h, state_pool, cache_ckv, cache_kpe,
           ada_w, ada_b, norm1_g, norm2_g, rec_w_in, rec_conv_w, rec_conv_b, rec_w_a, rec_b_a, rec_w_x, rec_b_x,
           rec_lambda, pool_w, pool_scale, rec_w_out, att_w_dq, att_g_q, att_w_uq, att_w_dkv, att_g_kv, att_w_ukv,
           att_w_out, peer_w_q, peer_keys, peer_u, peer_v, final_g):
    w = dict(ada_w=ada_w, ada_b=ada_b, norm1_g=norm1_g, norm2_g=norm2_g, rec_w_in=rec_w_in, rec_conv_w=rec_conv_w,
             rec_conv_b=rec_conv_b, rec_w_a=rec_w_a, rec_b_a=rec_b_a, rec_w_x=rec_w_x, rec_b_x=rec_b_x,
             rec_lambda=rec_lambda, pool_w=pool_w, pool_scale=pool_scale, rec_w_out=rec_w_out, att_w_dq=att_w_dq,
             att_g_q=att_g_q, att_w_uq=att_w_uq, att_w_dkv=att_w_dkv, att_g_kv=att_g_kv, att_w_ukv=att_w_ukv,
             att_w_out=att_w_out, peer_w_q=peer_w_q, peer_keys=peer_keys, peer_u=peer_u, peer_v=peer_v,
             final_g=final_g)
    p = _prep_weights(w)
    bp, tp, d = x_prompt.shape
    bs, ts, _ = x_sample.shape
    past = cache_ckv.shape[2]
    depth = ada_w.shape[0]
    rows = bp + bs
    rows_pad = -(-rows // 8) * 8
    c_all = jnp.pad(jnp.concatenate([c_prompt, c_sample], axis=0), ((0, rows_pad - rows), (0, 0)))
    mod = _ada(c_all, ada_w, ada_b)

    def mods_of(lo, n):
        return [[mod[l, lo:lo + n, k * d:(k + 1) * d][:, None, :] for k in range(6)] for l in range(depth)]

    zeros = lambda *s: jnp.zeros(s, F32)
    out_p = _trunk(x_prompt, mods_of(0, bp), 0, zeros(bp, CONV_WIDTH - 1, LRU_WIDTH), zeros(bp, LRU_WIDTH),
                   zeros(bp, POOL_MAX - 1, POOL_WIDTH), zeros(bp, 0, KV_LORA), zeros(bp, 0, ROPE_DIM), w, p)
    out_s = _trunk(x_sample, mods_of(bp, bs), past, state_conv[0], state_lru_h[0], state_pool[0],
                   cache_ckv[0], cache_kpe[0], w, p)
    return (out_p[0], out_s[0], out_p[1], out_s[1], out_p[2], out_s[2],
            out_p[3], out_s[3], out_p[4], out_s[4], out_p[5], out_s[5])
```

```python
import functools
import math

import jax
import jax.numpy as jnp
from jax import lax
from jax.experimental import pallas as pl
from jax.experimental.pallas import tpu as pltpu

F32 = jnp.float32
BF16 = jnp.bfloat16

D_MODEL = 2048
CHUNK = 64
RMS_EPS = 1e-6
LRU_WIDTH = 1024
LRU_HEADS = 8
LRU_HEAD_DIM = 128
CONV_WIDTH = 4
LRU_C = 8.0
POOL_WIDTH = 1024
POOL_WINDOWS = (2, 4, 8, 16)
POOL_GROUP_DIM = 256
POOL_MAX = 16
MIX_IN = 3072
MLA_HEADS = 16
Q_LORA = 512
KV_LORA = 256
NOPE_DIM = 128
ROPE_DIM = 64
V_DIM = 128
QK_DIM = NOPE_DIM + ROPE_DIM
ROPE_THETA = 10000.0
PEER_HEADS = 8
N_KEYS = 128
N_EXPERTS = N_KEYS * N_KEYS
PEER_TOPK = 16
HALF_KEY = 128

LANES = 128
BF16_ROWS = 16
EXPERT_BLOCK = 1024
EXPERT_SUB = 256
QK_PAD = 256
ATTN_TILE = 512
SOFTMAX_ROWS = 64
Q_SCALE = QK_DIM ** -0.5 * math.log2(math.e)
NEG_BIG = -1e30
VMEM_LIMIT = 56 * 1024 * 1024


def _cparams(sem):
    return pltpu.CompilerParams(dimension_semantics=sem, vmem_limit_bytes=VMEM_LIMIT)


def _const_spec(shape):
    n = len(shape)
    return pl.BlockSpec(shape, lambda *_: (0,) * n)


def _seq_tiles(b, t):
    if t >= 512:
        return 1, 512
    sb = max(1, min(b, 512 // t))
    while b % sb:
        sb -= 1
    return sb, t


def _rmsnorm(x, g):
    return x * lax.rsqrt(jnp.mean(x * x, axis=-1, keepdims=True) + RMS_EPS) * g


def _norm_mod(x, g, sh, sc):
    return _rmsnorm(x, g) * (1.0 + sc) + sh


def _ada_kernel(c_ref, w_ref, b_ref, o_ref):
    c = c_ref[...]
    sc = (c * jax.nn.sigmoid(c)).astype(BF16)
    o_ref[0] = jnp.dot(sc, w_ref[0].astype(BF16), preferred_element_type=F32) + b_ref[0]


def _ada(c_all, ada_w, ada_b):
    depth, d, n = ada_w.shape
    rows = c_all.shape[0]
    tn = 1024
    return pl.pallas_call(
        _ada_kernel,
        grid=(depth, n // tn),
        in_specs=[pl.BlockSpec((rows, d), lambda l, j: (0, 0)),
                  pl.BlockSpec((1, d, tn), lambda l, j: (l, 0, j)),
                  pl.BlockSpec((1, 1, tn), lambda l, j: (l, 0, j))],
        out_specs=pl.BlockSpec((1, rows, tn), lambda l, j: (l, 0, j)),
        out_shape=jax.ShapeDtypeStruct((depth, rows, n), F32),
        compiler_params=_cparams(("arbitrary", "arbitrary")),
        name="ada_mod",
    )(c_all, ada_w, ada_b.reshape(depth, 1, n))


def _inproj_kernel(x_ref, g_ref, sh_ref, sc_ref, w_ref, o_ref):
    sb, tt, d = x_ref.shape
    hn = _norm_mod(x_ref[...], g_ref[...], sh_ref[...], sc_ref[...])
    z = jnp.dot(hn.reshape(sb * tt, d).astype(BF16), w_ref[...], preferred_element_type=F32)
    o_ref[...] = z.reshape(sb, tt, -1)


def _inproj(x, g, sh, sc, w):
    b, t, d = x.shape
    n = w.shape[1]
    sb, tt = _seq_tiles(b, t)
    tt = min(tt, 256)
    seq = pl.BlockSpec((sb, 1, d), lambda i, j: (i, 0, 0))
    return pl.pallas_call(
        _inproj_kernel,
        grid=(b // sb, t // tt),
        in_specs=[pl.BlockSpec((sb, tt, d), lambda i, j: (i, j, 0)),
                  _const_spec((1, d)), seq, seq, _const_spec((d, n))],
        out_specs=pl.BlockSpec((sb, tt, n), lambda i, j: (i, j, 0)),
        out_shape=jax.ShapeDtypeStruct((b, t, n), F32),
        compiler_params=_cparams(("arbitrary", "arbitrary")),
        name="l0_inproj",
    )(x, g, sh, sc, w)


def _shift_rows(x, d, fill):
    rolled = pltpu.roll(x, d, 0)
    rows = lax.broadcasted_iota(jnp.int32, x.shape, 0)
    return jnp.where(rows >= d, rolled, fill)


def _gelu_tanh(x):
    return 0.5 * x * (1.0 + jnp.tanh(math.sqrt(2.0 / math.pi) * (x + 0.044715 * (x * x * x))))


def _recmix_kernel(z_ref, conv0_ref, h0_ref, pool0_ref, cw_ref, cb_ref, wa_ref, ba_ref, wx_ref, bx_ref,
                   lam_ref, pw_ref, ps_ref, mix_ref, hlast_ref, ctail, ptail, hstate, ext, *, pos0):
    sb, tt, _ = z_ref.shape
    j = pl.program_id(1)

    @pl.when(j == 0)
    def _():
        ctail[...] = conv0_ref[...]
        ptail[...] = pool0_ref[...]
        hstate[...] = h0_ref[...]

    lam = lam_ref[...]
    neg = -lam
    softplus_neg_lam = jnp.maximum(neg, 0.0) + jnp.log1p(jnp.exp(-jnp.abs(neg)))
    pos = pos0 + j * tt + lax.broadcasted_iota(jnp.int32, (tt, 1), 0)

    def per_seq(s, carry):
        rec = z_ref[s, :, LRU_WIDTH:2 * LRU_WIDTH]
        ext[0:8, :] = ctail[s]
        ext[8:8 + tt, :] = rec
        xc = cb_ref[...] + sum(ext[5 + k:5 + k + tt, :] * cw_ref[k:k + 1, :] for k in range(CONV_WIDTH))
        ctail[s] = ext[tt:tt + 8, :]
        xcb = xc.astype(BF16)
        r_parts, i_parts = [], []
        for h in range(LRU_HEADS):
            sl = slice(h * LRU_HEAD_DIM, (h + 1) * LRU_HEAD_DIM)
            r_parts.append(jnp.dot(xcb[:, sl], wa_ref[h], preferred_element_type=F32))
            i_parts.append(jnp.dot(xcb[:, sl], wx_ref[h], preferred_element_type=F32))
        r = jax.nn.sigmoid(jnp.concatenate(r_parts, axis=-1) + ba_ref[...])
        i = jax.nn.sigmoid(jnp.concatenate(i_parts, axis=-1) + bx_ref[...])
        log_a = -LRU_C * r * softplus_neg_lam
        a = jnp.exp(log_a)
        th = jnp.tanh(log_a)
        mult = jnp.sqrt(-2.0 * th / (1.0 - th))
        mult = jnp.where(pos == 0, 1.0, mult)
        u = mult * (i * xc)
        acc_a, acc_b = a, u
        d = 1
        while d < tt:
            a_sh = _shift_rows(acc_a, d, 1.0)
            b_sh = _shift_rows(acc_b, d, 0.0)
            acc_b = acc_a * b_sh + acc_b
            acc_a = acc_a * a_sh
            d *= 2
        hseq = acc_a * hstate[s] + acc_b
        hstate[s] = hseq[tt - 1:tt, :]
        gate = z_ref[s, :, 0:LRU_WIDTH]
        mix_ref[s, :, 0:LRU_WIDTH] = (_gelu_tanh(gate) * hseq).astype(mix_ref.dtype)
        pin = z_ref[s, :, 2 * LRU_WIDTH:]
        ext[0:16, :] = ptail[s]
        ext[16:16 + tt, :] = pin
        ptail[s] = ext[tt:tt + 16, :]
        for g, w in enumerate(POOL_WINDOWS):
            sl = slice(g * POOL_GROUP_DIM, (g + 1) * POOL_GROUP_DIM)
            win = sum(ext[16 - k:16 - k + tt, sl] for k in range(w))
            cnt = jnp.minimum(pos + 1, w).astype(F32)
            pooled = win / cnt - pin[:, sl]
            bo = jnp.dot(pooled.astype(BF16), pw_ref[g], preferred_element_type=F32) * ps_ref[:, sl]
            mix_ref[s, :, LRU_WIDTH + g * POOL_GROUP_DIM:LRU_WIDTH + (g + 1) * POOL_GROUP_DIM] = bo.astype(mix_ref.dtype)
        return carry

    lax.fori_loop(0, sb, per_seq, 0)

    @pl.when(j == pl.num_programs(1) - 1)
    def _():
        hlast_ref[...] = hstate[...]


def _recmix(z, conv0, h0, pool0, cw, cb, wa, ba, wx, bx, lam, pw, ps, pos0):
    b, t, _ = z.shape
    sb, tt = _seq_tiles(b, t)
    tt = min(tt, 256)
    c = LRU_WIDTH
    row = _const_spec((1, c))
    return pl.pallas_call(
        functools.partial(_recmix_kernel, pos0=pos0),
        grid=(b // sb, t // tt),
        in_specs=[pl.BlockSpec((sb, tt, MIX_IN), lambda i, j: (i, j, 0)),
                  pl.BlockSpec((sb, 8, c), lambda i, j: (i, 0, 0)),
                  pl.BlockSpec((sb, 1, c), lambda i, j: (i, 0, 0)),
                  pl.BlockSpec((sb, 16, c), lambda i, j: (i, 0, 0)),
                  _const_spec((CONV_WIDTH, c)), row,
                  _const_spec((LRU_HEADS, LRU_HEAD_DIM, LRU_HEAD_DIM)), row,
                  _const_spec((LRU_HEADS, LRU_HEAD_DIM, LRU_HEAD_DIM)), row, row,
                  _const_spec((len(POOL_WINDOWS), POOL_GROUP_DIM, POOL_GROUP_DIM)), row],
        out_specs=[pl.BlockSpec((sb, tt, 2 * c), lambda i, j: (i, j, 0)),
                   pl.BlockSpec((sb, 1, c), lambda i, j: (i, 0, 0))],
        out_shape=[jax.ShapeDtypeStruct((b, t, 2 * c), BF16),
                   jax.ShapeDtypeStruct((b, 1, c), F32)],
        scratch_shapes=[pltpu.VMEM((sb, 8, c), F32), pltpu.VMEM((sb, 16, c), F32),
                        pltpu.VMEM((sb, 1, c), F32), pltpu.VMEM((tt + 16, c), F32)],
        compiler_params=_cparams(("arbitrary", "arbitrary")),
        name="l0_recmix",
    )(z, conv0, h0, pool0, cw, cb, wa, ba, wx, bx, lam, pw, ps)


def _proj_res_kernel(a_ref, w_ref, x_ref, g1_ref, n2_ref, sh_ref, sc_ref, xo_ref, ho_ref):
    sb, tt, k = a_ref.shape
    mix = jnp.dot(a_ref[...].reshape(sb * tt, k), w_ref[...], preferred_element_type=F32)
    xn = x_ref[...] + g1_ref[...] * mix.reshape(sb, tt, -1)
    xo_ref[...] = xn
    ho_ref[...] = _norm_mod(xn, n2_ref[...], sh_ref[...], sc_ref[...]).astype(ho_ref.dtype)


def _proj_res(a, w, x, g1, n2, sh2, sc2):
    b, t, d = x.shape
    k = a.shape[-1]
    sb, tt = _seq_tiles(b, t)
    seq = pl.BlockSpec((sb, 1, d), lambda i, j: (i, 0, 0))
    tile = pl.BlockSpec((sb, tt, d), lambda i, j: (i, j, 0))
    return pl.pallas_call(
        _proj_res_kernel,
        grid=(b // sb, t // tt),
        in_specs=[pl.BlockSpec((sb, tt, k), lambda i, j: (i, j, 0)), _const_spec((k, d)), tile,
                  seq, _const_spec((1, d)), seq, seq],
        out_specs=[tile, tile],
        out_shape=[jax.ShapeDtypeStruct((b, t, d), F32), jax.ShapeDtypeStruct((b, t, d), BF16)],
        compiler_params=_cparams(("arbitrary", "arbitrary")),
        name="proj_res_norm",
    )(a, w, x, g1, n2, sh2, sc2)


def _top16_rows(vals):
    tm = vals.shape[1]
    rows16 = lax.broadcasted_iota(jnp.int32, (PEER_TOPK, tm), 0)
    top = jnp.zeros((PEER_TOPK, tm), F32)
    for k in range(PEER_TOPK):
        mx = jnp.max(vals, axis=0, keepdims=True)
        top = jnp.where(rows16 == k, mx, top)
        vals = jnp.where(vals == mx, -jnp.inf, vals)
    return top


def _bf16_product(x, y):
    return (x.astype(BF16) * y.astype(BF16)).astype(F32)


def _bf16_twice(x):
    hi = pltpu.bitcast(x.astype(BF16).astype(F32), jnp.uint32)
    return hi | (hi >> 16)


def _rank_pairs(t1, t2, op, fill):
    tm = t1.shape[1]
    row = lax.broadcasted_iota(jnp.int32, (PEER_TOPK, tm), 0)

    def put(slab, val, r0, n):
        return jnp.where(row < r0, slab, jnp.where(row < r0 + n, val, slab))

    def a_with_prefix(slab, a, r0, n):
        t2s = pltpu.roll(t2, r0, 0) if r0 else t2
        return put(slab, op(t1[a:a + 1, :], t2s), r0, n)

    blank = jnp.full((PEER_TOPK, tm), fill, F32)
    slab0 = op(t1[0:1, :], t2)
    slab1 = a_with_prefix(a_with_prefix(a_with_prefix(blank, 1, 0, 8), 2, 8, 5), 4, 13, 3)
    slab2 = blank
    for a, r0, n in ((3, 0, 4), (5, 4, 2), (6, 6, 2), (7, 8, 2)):
        slab2 = a_with_prefix(slab2, a, r0, n)
    tail = op(pltpu.roll(t1, 2, 0), t2[0:1, :])
    slab2 = put(slab2, tail, 10, 6)
    slab3 = put(blank, tail, 0, 2)
    return jnp.concatenate([slab0, slab1, slab2, slab3], axis=0)


def _peer_route_kernel(x_ref, wqt_ref, keys_ref, e1_ref, e2_ref, th_ref, qt_ref):
    qt_ref[...] = lax.dot_general(wqt_ref[...], x_ref[...], (((1,), (1,)), ((), ())),
                                  preferred_element_type=F32).astype(BF16)

    def per_head(h):
        r1 = pl.multiple_of(h * 2 * HALF_KEY, 2 * HALF_KEY)
        r2 = pl.multiple_of(h * 2 * HALF_KEY + HALF_KEY, HALF_KEY)
        s1 = jnp.dot(keys_ref[2 * h], qt_ref[pl.ds(r1, HALF_KEY), :], preferred_element_type=F32)
        s2 = jnp.dot(keys_ref[2 * h + 1], qt_ref[pl.ds(r2, HALF_KEY), :], preferred_element_type=F32)
        t1 = _top16_rows(s1)
        t2 = _top16_rows(s2)
        cand = _rank_pairs(t1, t2, jnp.add, -jnp.inf)
        vals = cand
        tau = None
        for _ in range(PEER_TOPK):
            tau = jnp.max(vals, axis=0, keepdims=True)
            vals = jnp.where(vals == tau, -jnp.inf, vals)
        sel = cand >= tau
        m1 = t1[0:1, :]
        m2 = t2[0:1, :]
        z = jnp.sum(jnp.where(sel, jnp.exp(cand - (m1 + m2)), 0.0), axis=0, keepdims=True)
        rz = 1.0 / z
        e1_ref[h] = _bf16_twice(jnp.exp(s1 - m1) * rz)
        e2_ref[h] = pltpu.bitcast(jnp.exp(s2 - m2).astype(BF16), jnp.uint32)
        prod = _rank_pairs(jnp.exp(t1 - m1) * rz, jnp.exp(t2 - m2), _bf16_product, jnp.inf)
        th_ref[h] = _bf16_twice(jnp.min(jnp.where(sel, prod, jnp.inf), axis=0, keepdims=True))

    def head_pair(hp, carry):
        per_head(2 * hp)
        per_head(2 * hp + 1)
        return carry

    lax.fori_loop(0, PEER_HEADS // 2, head_pair, 0)


def _peer_route(hn, wqt, keys):
    m, d = hn.shape
    tm = 256
    return pl.pallas_call(
        _peer_route_kernel,
        grid=(m // tm,),
        in_specs=[pl.BlockSpec((tm, d), lambda i: (i, 0)), _const_spec((d, d)),
                  _const_spec((2 * PEER_HEADS, N_KEYS, HALF_KEY))],
        out_specs=[pl.BlockSpec((PEER_HEADS, N_KEYS, tm), lambda i: (0, 0, i)),
                   pl.BlockSpec((PEER_HEADS, N_KEYS // 2, tm), lambda i: (0, 0, i)),
                   pl.BlockSpec((PEER_HEADS, 1, tm), lambda i: (0, 0, i))],
        out_shape=[jax.ShapeDtypeStruct((PEER_HEADS, N_KEYS, m), jnp.uint32),
                   jax.ShapeDtypeStruct((PEER_HEADS, N_KEYS // 2, m), jnp.uint32),
                   jax.ShapeDtypeStruct((PEER_HEADS, 1, m), jnp.uint32)],
        scratch_shapes=[pltpu.VMEM((d, tm), BF16)],
        compiler_params=_cparams(("arbitrary",)),
        name="peer_route",
    )(hn, wqt, keys)


def _gelu_erf(x):
    return 0.5 * x * (1.0 + lax.erf(x * math.sqrt(0.5)))


def _peer_expert_kernel(h_ref, u_ref, vt_ref, e1_ref, e2_ref, th_ref, x_ref, g2_ref, *rest, final_norm):
    if final_norm:
        fg_ref, o_ref, acc_ref, s_ref, a_ref = rest
    else:
        o_ref, acc_ref, s_ref, a_ref = rest
    e = pl.program_id(2)
    eb, tm = s_ref.shape
    sb_, tt, d = h_ref.shape
    hb = h_ref[...].reshape(tm, d)

    @pl.when(e == 0)
    def _():
        acc_ref[...] = jnp.zeros_like(acc_ref)

    for sb in range(eb // EXPERT_SUB):
        rows = slice(sb * EXPERT_SUB, (sb + 1) * EXPERT_SUB)
        s_ref[rows, :] = lax.dot_general(u_ref[rows, :], hb, (((1,), (1,)), ((), ())),
                                         preferred_element_type=F32)

    def row_tile(row):
        packed = pltpu.bitcast(jnp.broadcast_to(row, (BF16_ROWS // 2, LANES)), BF16)
        return jnp.concatenate([packed] * (N_KEYS // BF16_ROWS), axis=0)

    zero = jnp.zeros((N_KEYS, LANES), BF16)
    for ii in range(eb // N_KEYS):
        rows = slice(ii * N_KEYS, (ii + 1) * N_KEYS)
        for lt in range(tm // LANES):
            cols = slice(lt * LANES, (lt + 1) * LANES)
            g = zero
            for h in range(PEER_HEADS):
                p = pltpu.bitcast(e2_ref[h, :, cols], BF16) * row_tile(e1_ref[h, ii:ii + 1, cols])
                g = g + jnp.where(p >= row_tile(th_ref[h, :, cols]), p, zero)
            a_ref[rows, cols] = _gelu_erf(s_ref[rows, cols]).astype(BF16) * g
    acc_ref[...] += jnp.dot(vt_ref[0], a_ref[...], preferred_element_type=F32)

    @pl.when(e == pl.num_programs(2) - 1)
    def _():
        y = x_ref[...] + g2_ref[...] * acc_ref[...].T.reshape(sb_, tt, d)
        o_ref[...] = _rmsnorm(y, fg_ref[...]) if final_norm else y


def _peer_experts(hn, u, vt, e1, e2, th, x, g2, final_g):
    b, t, d = hn.shape
    sb, tt = _seq_tiles(b, t)
    tm = sb * tt
    nt = t // tt
    eb = EXPERT_BLOCK
    tile = pl.BlockSpec((sb, tt, d), lambda i, j, e: (i, j, 0))
    in_specs = [tile,
                pl.BlockSpec((eb, d), lambda i, j, e: (e, 0)),
                pl.BlockSpec((1, d, eb), lambda i, j, e: (e, 0, 0)),
                pl.BlockSpec((PEER_HEADS, eb // N_KEYS, tm), lambda i, j, e: (0, e, i * nt + j)),
                pl.BlockSpec((PEER_HEADS, N_KEYS // 2, tm), lambda i, j, e: (0, 0, i * nt + j)),
                pl.BlockSpec((PEER_HEADS, 1, tm), lambda i, j, e: (0, 0, i * nt + j)),
                tile,
                pl.BlockSpec((sb, 1, d), lambda i, j, e: (i, 0, 0))]
    args = [hn, u, vt, e1, e2, th, x, g2]
    if final_g is not None:
        in_specs.append(_const_spec((1, d)))
        args.append(final_g)
    return pl.pallas_call(
        functools.partial(_peer_expert_kernel, final_norm=final_g is not None),
        grid=(b // sb, nt, N_EXPERTS // eb),
        in_specs=in_specs,
        out_specs=tile,
        out_shape=jax.ShapeDtypeStruct((b, t, d), F32),
        scratch_shapes=[pltpu.VMEM((d, tm), F32), pltpu.VMEM((eb, tm), F32), pltpu.VMEM((eb, tm), BF16)],
        compiler_params=_cparams(("arbitrary", "arbitrary", "arbitrary")),
        name="peer_experts",
    )(*args)


def _qkv_kernel(x_ref, g_ref, sh_ref, sc_ref, wdq_ref, gq_ref, wq1_ref, wq2_ref, wdkv_ref, gkv_ref,
                cos_ref, sin_ref, q_ref, ckv_ref, kpe_ref):
    sb, tt, d = x_ref.shape
    hn = _norm_mod(x_ref[...], g_ref[...], sh_ref[...], sc_ref[...]).reshape(sb * tt, d).astype(BF16)
    cq = (_rmsnorm(jnp.dot(hn, wdq_ref[...], preferred_element_type=F32), gq_ref[...]) * Q_SCALE).astype(BF16)
    q1 = jnp.dot(cq, wq1_ref[...], preferred_element_type=F32)
    q2 = jnp.dot(cq, wq2_ref[...], preferred_element_type=F32)
    cos = cos_ref[...][None]
    sin = sin_ref[...][None]
    for h in range(MLA_HEADS):
        q_ref[:, :, h * QK_PAD:h * QK_PAD + NOPE_DIM] = (
            q1[:, h * QK_PAD:h * QK_PAD + NOPE_DIM].reshape(sb, tt, NOPE_DIM).astype(q_ref.dtype))
        pe = (q1[:, h * QK_PAD + NOPE_DIM:(h + 1) * QK_PAD].reshape(sb, tt, LANES) * cos
              + q2[:, h * LANES:(h + 1) * LANES].reshape(sb, tt, LANES) * sin)
        q_ref[:, :, h * QK_PAD + NOPE_DIM:(h + 1) * QK_PAD] = pe.astype(q_ref.dtype)
    kv = jnp.dot(hn, wdkv_ref[...], preferred_element_type=F32)
    ckv_ref[...] = _rmsnorm(kv[:, :KV_LORA], gkv_ref[...]).reshape(sb, tt, KV_LORA)
    kpe = (kv[:, KV_LORA:KV_LORA + LANES].reshape(sb, tt, LANES) * cos
           + kv[:, KV_LORA + LANES:].reshape(sb, tt, LANES) * sin)
    kpe_ref[...] = kpe[:, :, :ROPE_DIM]


def _qkv(x, g, sh, sc, wdq, gq, wq1, wq2, wdkv, gkv, cos, sin):
    b, t, d = x.shape
    sb, tt = _seq_tiles(b, t)
    tt = min(tt, 256)
    seq = pl.BlockSpec((sb, 1, d), lambda i, j: (i, 0, 0))
    rope = pl.BlockSpec((tt, LANES), lambda i, j: (j, 0))
    return pl.pallas_call(
        _qkv_kernel,
        grid=(b // sb, t // tt),
        in_specs=[pl.BlockSpec((sb, tt, d), lambda i, j: (i, j, 0)), _const_spec((1, d)), seq, seq,
                  _const_spec(wdq.shape), _const_spec((1, Q_LORA)), _const_spec(wq1.shape), _const_spec(wq2.shape),
                  _const_spec(wdkv.shape), _const_spec((1, KV_LORA)), rope, rope],
        out_specs=[pl.BlockSpec((sb, tt, MLA_HEADS * QK_PAD), lambda i, j: (i, j, 0)),
                   pl.BlockSpec((sb, tt, KV_LORA), lambda i, j: (i, j, 0)),
                   pl.BlockSpec((sb, tt, ROPE_DIM), lambda i, j: (i, j, 0))],
        out_shape=[jax.ShapeDtypeStruct((b, t, MLA_HEADS * QK_PAD), BF16),
                   jax.ShapeDtypeStruct((b, t, KV_LORA), F32),
                   jax.ShapeDtypeStruct((b, t, ROPE_DIM), F32)],
        compiler_params=_cparams(("arbitrary", "arbitrary")),
        name="l1_qkv",
    )(x, g, sh, sc, wdq, gq, wq1, wq2, wdkv, gkv, cos, sin)


def _kvup_kernel(ckv_ref, kpe_ref, wk_ref, wv_ref, k_ref, v_ref):
    c = ckv_ref[0].astype(BF16)
    kn = jnp.dot(c, wk_ref[...], preferred_element_type=F32)
    v_ref[0] = jnp.dot(c, wv_ref[...], preferred_element_type=F32).astype(v_ref.dtype)
    kpe = kpe_ref[0].astype(k_ref.dtype)
    for h in range(MLA_HEADS):
        k_ref[0, :, h * QK_PAD:h * QK_PAD + NOPE_DIM] = kn[:, h * NOPE_DIM:(h + 1) * NOPE_DIM].astype(k_ref.dtype)
        k_ref[0, :, h * QK_PAD + NOPE_DIM:(h + 1) * QK_PAD] = kpe


def _kvup(ckv, kpe128, wk, wv):
    b, t, _ = ckv.shape
    tt = max(c for c in range(16, 513, 16) if t % c == 0)
    return pl.pallas_call(
        _kvup_kernel,
        grid=(b, t // tt),
        in_specs=[pl.BlockSpec((1, tt, KV_LORA), lambda i, j: (i, j, 0)),
                  pl.BlockSpec((1, tt, LANES), lambda i, j: (i, j, 0)),
                  _const_spec(wk.shape), _const_spec(wv.shape)],
        out_specs=[pl.BlockSpec((1, tt, MLA_HEADS * QK_PAD), lambda i, j: (i, j, 0)),
                   pl.BlockSpec((1, tt, MLA_HEADS * V_DIM), lambda i, j: (i, j, 0))],
        out_shape=[jax.ShapeDtypeStruct((b, t, MLA_HEADS * QK_PAD), BF16),
                   jax.ShapeDtypeStruct((b, t, MLA_HEADS * V_DIM), BF16)],
        compiler_params=_cparams(("arbitrary", "arbitrary")),
        name="l1_kvup",
    )(ckv, kpe128, wk, wv)


def _qk(q, k):
    return lax.dot_general(q, k, (((1,), (1,)), ((), ())), preferred_element_type=F32)


def _attn_stream_kernel(q_ref, k_ref, v_ref, bias_ref, o_ref, s0, s1, p0, p1, a0, a1, m_run, l_run, acc):
    t = q_ref.shape[1]
    qi = pl.program_id(2)
    q = q_ref[0]

    def scores(k):
        start = pl.multiple_of(jnp.minimum(k, qi) * t, t)
        which = jnp.where(k == qi, 1, jnp.where(k > qi, 2, 0))
        return _qk(q, k_ref[0, pl.ds(start, t), :]) + bias_ref[which]

    def values(k):
        start = pl.multiple_of(jnp.clip(k, 0, qi) * t, t)
        return v_ref[0, pl.ds(start, t), :]

    def half(k, s_cur, s_next, p_cur, p_prev, a_cur, a_prev):
        s_next[...] = scores(k + 1)
        acc[...] = a_prev[...] * acc[...] + jnp.dot(p_prev[...], values(k - 1), preferred_element_type=F32)
        m_old = m_run[...]
        m_new = jnp.maximum(m_old, jnp.max(s_cur[...], axis=-1, keepdims=True))
        alpha = jnp.exp2(m_old - m_new)
        a_cur[...] = alpha
        m_run[...] = m_new
        p = jnp.exp2(s_cur[...] - jnp.concatenate([m_new] * (t // LANES), axis=1))
        l_run[...] = alpha * l_run[...] + jnp.sum(p, axis=-1, keepdims=True)
        p_cur[...] = p.astype(p_cur.dtype)

    s0[...] = scores(0)
    p1[...] = jnp.zeros(p1.shape, p1.dtype)
    a1[...] = jnp.ones(a1.shape, F32)
    acc[...] = jnp.zeros(acc.shape, F32)
    m_run[...] = jnp.full(m_run.shape, NEG_BIG, F32)
    l_run[...] = jnp.zeros(l_run.shape, F32)

    def body(j, carry):
        half(2 * j, s0, s1, p0, p1, a0, a1)
        half(2 * j + 1, s1, s0, p1, p0, a1, a0)
        return carry

    pairs = (qi + 2) // 2
    lax.fori_loop(0, pairs, body, 0)
    out = a1[...] * acc[...] + jnp.dot(p1[...], values(2 * pairs - 1), preferred_element_type=F32)
    o_ref[0] = (out / l_run[...]).astype(o_ref.dtype)


def _attn_single_kernel(q_ref, k_ref, v_ref, o_ref, *, q_off):
    tq = q_ref.shape[1]
    tk = k_ref.shape[1]
    s = _qk(q_ref[0], k_ref[0])
    qc = (q_off + lax.broadcasted_iota(jnp.int32, (tq, tk), 0)) // CHUNK
    kc = lax.broadcasted_iota(jnp.int32, (tq, tk), 1) // CHUNK
    s = jnp.where(kc <= qc, s, NEG_BIG)
    p = jnp.exp2(s - jnp.max(s, axis=-1, keepdims=True))
    l = jnp.sum(p, axis=-1, keepdims=True)
    o_ref[0] = (jnp.dot(p.astype(BF16), v_ref[0], preferred_element_type=F32) / l).astype(o_ref.dtype)


def _attention(q, k, v):
    b, tq_all, _ = q.shape
    tk_all = k.shape[1]
    out_shape = jax.ShapeDtypeStruct((b, tq_all, MLA_HEADS * V_DIM), BF16)
    kv_specs = [pl.BlockSpec((1, tk_all, QK_PAD), lambda i, h, j: (i, 0, h)),
                pl.BlockSpec((1, tk_all, V_DIM), lambda i, h, j: (i, 0, h))]
    if tq_all == tk_all and tq_all % ATTN_TILE == 0:
        t = ATTN_TILE
        r = lax.broadcasted_iota(jnp.int32, (t, t), 0) // CHUNK
        c = lax.broadcasted_iota(jnp.int32, (t, t), 1) // CHUNK
        bias = jnp.stack([jnp.zeros((t, t), F32), jnp.where(c <= r, 0.0, NEG_BIG).astype(F32),
                          jnp.full((t, t), NEG_BIG, F32)])
        return pl.pallas_call(
            _attn_stream_kernel,
            grid=(b, MLA_HEADS, tq_all // t),
            in_specs=[pl.BlockSpec((1, t, QK_PAD), lambda i, h, j: (i, j, h))] + kv_specs + [_const_spec((3, t, t))],
            out_specs=pl.BlockSpec((1, t, V_DIM), lambda i, h, j: (i, j, h)),
            out_shape=out_shape,
            scratch_shapes=[pltpu.VMEM((t, t), F32), pltpu.VMEM((t, t), F32),
                            pltpu.VMEM((t, t), BF16), pltpu.VMEM((t, t), BF16),
                            pltpu.VMEM((t, LANES), F32), pltpu.VMEM((t, LANES), F32),
                            pltpu.VMEM((t, LANES), F32), pltpu.VMEM((t, LANES), F32), pltpu.VMEM((t, V_DIM), F32)],
            compiler_params=_cparams(("arbitrary", "arbitrary", "arbitrary")),
            name="l1_attention",
        )(q, k, v, bias)
    assert tq_all <= ATTN_TILE, "history + long query block is not a shape this trunk step has"
    return pl.pallas_call(
        functools.partial(_attn_single_kernel, q_off=tk_all - tq_all),
        grid=(b, MLA_HEADS, 1),
        in_specs=[pl.BlockSpec((1, tq_all, QK_PAD), lambda i, h, j: (i, 0, h))] + kv_specs,
        out_specs=pl.BlockSpec((1, tq_all, V_DIM), lambda i, h, j: (i, 0, h)),
        out_shape=out_shape,
        compiler_params=_cparams(("arbitrary", "arbitrary", "arbitrary")),
        name="l1_attention_step",
    )(q, k, v)


def _prep_weights(w):
    p = {}
    p["rec_w_in"] = w["rec_w_in"][0].astype(BF16)
    p["rec_w_a"] = w["rec_w_a"][0].astype(BF16)
    p["rec_w_x"] = w["rec_w_x"][0].astype(BF16)
    p["pool_w"] = w["pool_w"][0].astype(BF16)
    p["rec_w_out"] = w["rec_w_out"][0].astype(BF16)
    p["att_w_dq"] = w["att_w_dq"][0].astype(BF16)
    wuq = w["att_w_uq"][0].reshape(Q_LORA, MLA_HEADS, QK_DIM)
    half = ROPE_DIM // 2
    zpad = jnp.zeros((Q_LORA, MLA_HEADS, QK_PAD - QK_DIM), F32)
    p["wq1"] = jnp.concatenate([wuq, zpad], axis=-1).reshape(Q_LORA, MLA_HEADS * QK_PAD).astype(BF16)
    swapped = jnp.concatenate([wuq[..., NOPE_DIM + half:], wuq[..., NOPE_DIM:NOPE_DIM + half],
                               jnp.zeros((Q_LORA, MLA_HEADS, LANES - ROPE_DIM), F32)], axis=-1)
    p["wq2"] = swapped.reshape(Q_LORA, MLA_HEADS * LANES).astype(BF16)
    wdkv = w["att_w_dkv"][0]
    zl = jnp.zeros((D_MODEL, LANES - ROPE_DIM), F32)
    p["wdkv"] = jnp.concatenate([wdkv, zl, wdkv[:, KV_LORA + half:], wdkv[:, KV_LORA:KV_LORA + half], zl],
                                axis=-1).astype(BF16)
    wukv = w["att_w_ukv"][0].reshape(KV_LORA, MLA_HEADS, NOPE_DIM + V_DIM)
    p["wk"] = wukv[..., :NOPE_DIM].reshape(KV_LORA, MLA_HEADS * NOPE_DIM).astype(BF16)
    p["wv"] = wukv[..., NOPE_DIM:].reshape(KV_LORA, MLA_HEADS * V_DIM).astype(BF16)
    p["att_w_out"] = w["att_w_out"][0].astype(BF16)
    p["peer_wqt"] = [w["peer_w_q"][l].T.astype(BF16) for l in range(2)]
    p["peer_keys"] = [w["peer_keys"][l].reshape(2 * PEER_HEADS, N_KEYS, HALF_KEY).astype(BF16) for l in range(2)]
    p["peer_u"] = [w["peer_u"][l].astype(BF16) for l in range(2)]
    p["peer_vt"] = [w["peer_v"][l].astype(BF16).reshape(N_EXPERTS // EXPERT_BLOCK, EXPERT_BLOCK, D_MODEL)
                    .transpose(0, 2, 1) for l in range(2)]
    return p


def _rope_tables(pos):
    half = ROPE_DIM // 2
    inv = ROPE_THETA ** (-(jnp.arange(half, dtype=F32) / half))
    ang = pos.astype(F32)[:, None] * inv[None, :]
    cos = jnp.cos(ang)
    sin = jnp.sin(ang)
    z = jnp.zeros((pos.shape[0], LANES - ROPE_DIM), F32)
    return jnp.concatenate([cos, cos, z], axis=-1), jnp.concatenate([-sin, sin, z], axis=-1)


def _peer(hn, x, g2, p, l, final_g=None):
    b, t, d = hn.shape
    e1, e2, th = _peer_route(hn.reshape(b * t, d), p["peer_wqt"][l], p["peer_keys"][l])
    return _peer_experts(hn, p["peer_u"][l], p["peer_vt"][l], e1, e2, th, x, g2, final_g)


def _trunk(x, mods, pos0, conv_prev, h_prev, pool_prev, ckv_prev, kpe_prev, w, p):
    b, t, d = x.shape
    row = lambda a: a.reshape(1, -1)
    sh1, sc1, g1, sh2, sc2, g2 = mods[0]
    z = _inproj(x, row(w["norm1_g"][0]), sh1, sc1, p["rec_w_in"])
    conv0 = jnp.pad(conv_prev, ((0, 0), (8 - (CONV_WIDTH - 1), 0), (0, 0)))
    pool0 = jnp.pad(pool_prev, ((0, 0), (1, 0), (0, 0)))
    mix, h_last = _recmix(z, conv0, h_prev[:, None, :], pool0, w["rec_conv_w"][0], row(w["rec_conv_b"][0]),
                          p["rec_w_a"], row(w["rec_b_a"][0]), p["rec_w_x"], row(w["rec_b_x"][0]),
                          row(w["rec_lambda"][0]), p["pool_w"], row(w["pool_scale"][0]), pos0)
    new_conv = z[:, t - (CONV_WIDTH - 1):, LRU_WIDTH:2 * LRU_WIDTH]
    new_pool = z[:, t - (POOL_MAX - 1):, 2 * LRU_WIDTH:]
    x, hn = _proj_res(mix, p["rec_w_out"], x, g1, row(w["norm2_g"][0]), sh2, sc2)
    x = _peer(hn, x, g2, p, 0)
    sh1, sc1, g1, sh2, sc2, g2 = mods[1]
    pos = pos0 + jnp.arange(t, dtype=jnp.int32)
    cos, sin = _rope_tables(pos)
    q, ckv, kpe = _qkv(x, row(w["norm1_g"][1]), sh1, sc1, p["att_w_dq"], row(w["att_g_q"][0]), p["wq1"], p["wq2"],
                       p["wdkv"], row(w["att_g_kv"][0]), cos, sin)
    ckv_all = jnp.concatenate([ckv_prev, ckv], axis=1)
    kpe_all = jnp.concatenate([kpe_prev, kpe], axis=1)
    kpe128 = jnp.pad(kpe_all, ((0, 0), (0, 0), (0, LANES - ROPE_DIM)))
    k, v = _kvup(ckv_all, kpe128, p["wk"], p["wv"])
    o = _attention(q, k, v)
    x, hn = _proj_res(o, p["att_w_out"], x, g1, row(w["norm2_g"][1]), sh2, sc2)
    y = _peer(hn, x, g2, p, 1, row(w["final_g"]))
    return y, new_conv[None], h_last.reshape(1, b, LRU_WIDTH), new_pool[None], ckv[None], kpe[None]


def kernel(x_prompt, x_sample, c_prompt, c_sample, state_conv, state_lru_h, state_pool, cache_ckv, cache_kpe,
           ada_w, ada_b, norm1_g, norm2_g, rec_w_in, rec_conv_w, rec_conv_b, rec_w_a, rec_b_a, rec_w_x, rec_b_x,
           rec_lambda, pool_w, pool_scale, rec_w_out, att_w_dq, att_g_q, att_w_uq, att_w_dkv, att_g_kv, att_w_ukv,
           att_w_out, peer_w_q, peer_keys, peer_u, peer_v, final_g):
    w = dict(ada_w=ada_w, ada_b=ada_b, norm1_g=norm1_g, norm2_g=norm2_g, rec_w_in=rec_w_in, rec_conv_w=rec_conv_w,
             rec_conv_b=rec_conv_b, rec_w_a=rec_w_a, rec_b_a=rec_b_a, rec_w_x=rec_w_x, rec_b_x=rec_b_x,
             rec_lambda=rec_lambda, pool_w=pool_w, pool_scale=pool_scale, rec_w_out=rec_w_out, att_w_dq=att_w_dq,
             att_g_q=att_g_q, att_w_uq=att_w_uq, att_w_dkv=att_w_dkv, att_g_kv=att_g_kv, att_w_ukv=att_w_ukv,
             att_w_out=att_w_out, peer_w_q=peer_w_q, peer_keys=peer_keys, peer_u=peer_u, peer_v=peer_v,
             final_g=final_g)
    p = _prep_weights(w)
    bp, tp, d = x_prompt.shape
    bs, ts, _ = x_sample.shape
    past = cache_ckv.shape[2]
    depth = ada_w.shape[0]
    rows = bp + bs
    rows_pad = -(-rows // 8) * 8
    c_all = jnp.pad(jnp.concatenate([c_prompt, c_sample], axis=0), ((0, rows_pad - rows), (0, 0)))
    mod = _ada(c_all, ada_w, ada_b)

    def mods_of(lo, n):
        return [[mod[l, lo:lo + n, k * d:(k + 1) * d][:, None, :] for k in range(6)] for l in range(depth)]

    zeros = lambda *s: jnp.zeros(s, F32)
    out_p = _trunk(x_prompt, mods_of(0, bp), 0, zeros(bp, CONV_WIDTH - 1, LRU_WIDTH), zeros(bp, LRU_WIDTH),
                   zeros(bp, POOL_MAX - 1, POOL_WIDTH), zeros(bp, 0, KV_LORA), zeros(bp, 0, ROPE_DIM), w, p)
    out_s = _trunk(x_sample, mods_of(bp, bs), past, state_conv[0], state_lru_h[0], state_pool[0],
                   cache_ckv[0], cache_kpe[0], w, p)
    return (out_p[0], out_s[0], out_p[1], out_s[1], out_p[2], out_s[2],
            out_p[3], out_s[3], out_p[4], out_s[4], out_p[5], out_s[5])
```

```python
import functools
import math

import jax
import jax.numpy as jnp
from jax import lax
from jax.experimental import pallas as pl
from jax.experimental.pallas import tpu as pltpu

F32 = jnp.float32
BF16 = jnp.bfloat16

D_MODEL = 2048
CHUNK = 64
RMS_EPS = 1e-6
LRU_WIDTH = 1024
LRU_HEADS = 8
LRU_HEAD_DIM = 128
CONV_WIDTH = 4
LRU_C = 8.0
POOL_WIDTH = 1024
POOL_WINDOWS = (2, 4, 8, 16)
POOL_GROUP_DIM = 256
POOL_MAX = 16
MIX_IN = 3072
MLA_HEADS = 16
Q_LORA = 512
KV_LORA = 256
NOPE_DIM = 128
ROPE_DIM = 64
V_DIM = 128
QK_DIM = NOPE_DIM + ROPE_DIM
ROPE_THETA = 10000.0
PEER_HEADS = 8
N_KEYS = 128
N_EXPERTS = N_KEYS * N_KEYS
PEER_TOPK = 16
HALF_KEY = 128

LANES = 128
BF16_ROWS = 16
EXPERT_BLOCK = 1024
EXPERT_SUB = 256
QK_PAD = 256
V_PAD = 256
MASK_COL = QK_DIM
ATTN_TILE = 512
SOFTMAX_ROWS = 64
Q_SCALE = QK_DIM ** -0.5 * math.log2(math.e)
NEG_BIG = -1e30
VMEM_LIMIT = 56 * 1024 * 1024


def _cparams(sem):
    return pltpu.CompilerParams(dimension_semantics=sem, vmem_limit_bytes=VMEM_LIMIT)


def _const_spec(shape):
    n = len(shape)
    return pl.BlockSpec(shape, lambda *_: (0,) * n)


def _seq_tiles(b, t):
    if t >= 512:
        return 1, 512
    sb = max(1, min(b, 512 // t))
    while b % sb:
        sb -= 1
    return sb, t


def _rmsnorm(x, g):
    return x * lax.rsqrt(jnp.mean(x * x, axis=-1, keepdims=True) + RMS_EPS) * g


def _norm_mod(x, g, sh, sc):
    return _rmsnorm(x, g) * (1.0 + sc) + sh


def _ada_kernel(c_ref, w_ref, b_ref, o_ref):
    c = c_ref[...]
    sc = (c * jax.nn.sigmoid(c)).astype(BF16)
    o_ref[0] = jnp.dot(sc, w_ref[0].astype(BF16), preferred_element_type=F32) + b_ref[0]


def _ada(c_all, ada_w, ada_b):
    depth, d, n = ada_w.shape
    rows = c_all.shape[0]
    tn = 1024
    return pl.pallas_call(
        _ada_kernel,
        grid=(depth, n // tn),
        in_specs=[pl.BlockSpec((rows, d), lambda l, j: (0, 0)),
                  pl.BlockSpec((1, d, tn), lambda l, j: (l, 0, j)),
                  pl.BlockSpec((1, 1, tn), lambda l, j: (l, 0, j))],
        out_specs=pl.BlockSpec((1, rows, tn), lambda l, j: (l, 0, j)),
        out_shape=jax.ShapeDtypeStruct((depth, rows, n), F32),
        compiler_params=_cparams(("arbitrary", "arbitrary")),
        name="ada_mod",
    )(c_all, ada_w, ada_b.reshape(depth, 1, n))


def _inproj_kernel(x_ref, g_ref, sh_ref, sc_ref, w_ref, o_ref):
    sb, tt, d = x_ref.shape
    hn = _norm_mod(x_ref[...], g_ref[...], sh_ref[...], sc_ref[...])
    z = jnp.dot(hn.reshape(sb * tt, d).astype(BF16), w_ref[...], preferred_element_type=F32)
    o_ref[...] = z.reshape(sb, tt, -1)


def _inproj(x, g, sh, sc, w):
    b, t, d = x.shape
    n = w.shape[1]
    sb, tt = _seq_tiles(b, t)
    tt = min(tt, 256)
    seq = pl.BlockSpec((sb, 1, d), lambda i, j: (i, 0, 0))
    return pl.pallas_call(
        _inproj_kernel,
        grid=(b // sb, t // tt),
        in_specs=[pl.BlockSpec((sb, tt, d), lambda i, j: (i, j, 0)),
                  _const_spec((1, d)), seq, seq, _const_spec((d, n))],
        out_specs=pl.BlockSpec((sb, tt, n), lambda i, j: (i, j, 0)),
        out_shape=jax.ShapeDtypeStruct((b, t, n), F32),
        compiler_params=_cparams(("arbitrary", "arbitrary")),
        name="l0_inproj",
    )(x, g, sh, sc, w)


def _shift_rows(x, d, fill):
    rolled = pltpu.roll(x, d, 0)
    rows = lax.broadcasted_iota(jnp.int32, x.shape, 0)
    return jnp.where(rows >= d, rolled, fill)


def _gelu_tanh(x):
    return 0.5 * x * (1.0 + jnp.tanh(math.sqrt(2.0 / math.pi) * (x + 0.044715 * (x * x * x))))


def _recmix_kernel(z_ref, conv0_ref, h0_ref, pool0_ref, cw_ref, cb_ref, wa_ref, ba_ref, wx_ref, bx_ref,
                   lam_ref, pw_ref, ps_ref, mix_ref, hlast_ref, ctail, ptail, hstate, ext, *, pos0):
    sb, tt, _ = z_ref.shape
    j = pl.program_id(1)

    @pl.when(j == 0)
    def _():
        ctail[...] = conv0_ref[...]
        ptail[...] = pool0_ref[...]
        hstate[...] = h0_ref[...]

    lam = lam_ref[...]
    neg = -lam
    softplus_neg_lam = jnp.maximum(neg, 0.0) + jnp.log1p(jnp.exp(-jnp.abs(neg)))
    pos = pos0 + j * tt + lax.broadcasted_iota(jnp.int32, (tt, 1), 0)

    def per_seq(s, carry):
        rec = z_ref[s, :, LRU_WIDTH:2 * LRU_WIDTH]
        ext[0:8, :] = ctail[s]
        ext[8:8 + tt, :] = rec
        xc = cb_ref[...] + sum(ext[5 + k:5 + k + tt, :] * cw_ref[k:k + 1, :] for k in range(CONV_WIDTH))
        ctail[s] = ext[tt:tt + 8, :]
        xcb = xc.astype(BF16)
        r_parts, i_parts = [], []
        for h in range(LRU_HEADS):
            sl = slice(h * LRU_HEAD_DIM, (h + 1) * LRU_HEAD_DIM)
            r_parts.append(jnp.dot(xcb[:, sl], wa_ref[h], preferred_element_type=F32))
            i_parts.append(jnp.dot(xcb[:, sl], wx_ref[h], preferred_element_type=F32))
        r = jax.nn.sigmoid(jnp.concatenate(r_parts, axis=-1) + ba_ref[...])
        i = jax.nn.sigmoid(jnp.concatenate(i_parts, axis=-1) + bx_ref[...])
        log_a = -LRU_C * r * softplus_neg_lam
        a = jnp.exp(log_a)
        th = jnp.tanh(log_a)
        mult = jnp.sqrt(-2.0 * th / (1.0 - th))
        mult = jnp.where(pos == 0, 1.0, mult)
        u = mult * (i * xc)
        acc_a, acc_b = a, u
        d = 1
        while d < tt:
            a_sh = _shift_rows(acc_a, d, 1.0)
            b_sh = _shift_rows(acc_b, d, 0.0)
            acc_b = acc_a * b_sh + acc_b
            acc_a = acc_a * a_sh
            d *= 2
        hseq = acc_a * hstate[s] + acc_b
        hstate[s] = hseq[tt - 1:tt, :]
        gate = z_ref[s, :, 0:LRU_WIDTH]
        mix_ref[s, :, 0:LRU_WIDTH] = (_gelu_tanh(gate) * hseq).astype(mix_ref.dtype)
        pin = z_ref[s, :, 2 * LRU_WIDTH:]
        ext[0:16, :] = ptail[s]
        ext[16:16 + tt, :] = pin
        ptail[s] = ext[tt:tt + 16, :]
        for g, w in enumerate(POOL_WINDOWS):
            sl = slice(g * POOL_GROUP_DIM, (g + 1) * POOL_GROUP_DIM)
            win = sum(ext[16 - k:16 - k + tt, sl] for k in range(w))
            cnt = jnp.minimum(pos + 1, w).astype(F32)
            pooled = win / cnt - pin[:, sl]
            bo = jnp.dot(pooled.astype(BF16), pw_ref[g], preferred_element_type=F32) * ps_ref[:, sl]
            mix_ref[s, :, LRU_WIDTH + g * POOL_GROUP_DIM:LRU_WIDTH + (g + 1) * POOL_GROUP_DIM] = bo.astype(mix_ref.dtype)
        return carry

    lax.fori_loop(0, sb, per_seq, 0)

    @pl.when(j == pl.num_programs(1) - 1)
    def _():
        hlast_ref[...] = hstate[...]


def _recmix(z, conv0, h0, pool0, cw, cb, wa, ba, wx, bx, lam, pw, ps, pos0):
    b, t, _ = z.shape
    sb, tt = _seq_tiles(b, t)
    tt = min(tt, 256)
    c = LRU_WIDTH
    row = _const_spec((1, c))
    return pl.pallas_call(
        functools.partial(_recmix_kernel, pos0=pos0),
        grid=(b // sb, t // tt),
        in_specs=[pl.BlockSpec((sb, tt, MIX_IN), lambda i, j: (i, j, 0)),
                  pl.BlockSpec((sb, 8, c), lambda i, j: (i, 0, 0)),
                  pl.BlockSpec((sb, 1, c), lambda i, j: (i, 0, 0)),
                  pl.BlockSpec((sb, 16, c), lambda i, j: (i, 0, 0)),
                  _const_spec((CONV_WIDTH, c)), row,
                  _const_spec((LRU_HEADS, LRU_HEAD_DIM, LRU_HEAD_DIM)), row,
                  _const_spec((LRU_HEADS, LRU_HEAD_DIM, LRU_HEAD_DIM)), row, row,
                  _const_spec((len(POOL_WINDOWS), POOL_GROUP_DIM, POOL_GROUP_DIM)), row],
        out_specs=[pl.BlockSpec((sb, tt, 2 * c), lambda i, j: (i, j, 0)),
                   pl.BlockSpec((sb, 1, c), lambda i, j: (i, 0, 0))],
        out_shape=[jax.ShapeDtypeStruct((b, t, 2 * c), BF16),
                   jax.ShapeDtypeStruct((b, 1, c), F32)],
        scratch_shapes=[pltpu.VMEM((sb, 8, c), F32), pltpu.VMEM((sb, 16, c), F32),
                        pltpu.VMEM((sb, 1, c), F32), pltpu.VMEM((tt + 16, c), F32)],
        compiler_params=_cparams(("arbitrary", "arbitrary")),
        name="l0_recmix",
    )(z, conv0, h0, pool0, cw, cb, wa, ba, wx, bx, lam, pw, ps)


def _proj_res_kernel(a_ref, w_ref, x_ref, g1_ref, n2_ref, sh_ref, sc_ref, xo_ref, ho_ref):
    sb, tt, k = a_ref.shape
    mix = jnp.dot(a_ref[...].reshape(sb * tt, k), w_ref[...], preferred_element_type=F32)
    xn = x_ref[...] + g1_ref[...] * mix.reshape(sb, tt, -1)
    xo_ref[...] = xn
    ho_ref[...] = _norm_mod(xn, n2_ref[...], sh_ref[...], sc_ref[...]).astype(ho_ref.dtype)


def _proj_res(a, w, x, g1, n2, sh2, sc2):
    b, t, d = x.shape
    k = a.shape[-1]
    sb, tt = _seq_tiles(b, t)
    seq = pl.BlockSpec((sb, 1, d), lambda i, j: (i, 0, 0))
    tile = pl.BlockSpec((sb, tt, d), lambda i, j: (i, j, 0))
    return pl.pallas_call(
        _proj_res_kernel,
        grid=(b // sb, t // tt),
        in_specs=[pl.BlockSpec((sb, tt, k), lambda i, j: (i, j, 0)), _const_spec((k, d)), tile,
                  seq, _const_spec((1, d)), seq, seq],
        out_specs=[tile, tile],
        out_shape=[jax.ShapeDtypeStruct((b, t, d), F32), jax.ShapeDtypeStruct((b, t, d), BF16)],
        compiler_params=_cparams(("arbitrary", "arbitrary")),
        name="proj_res_norm",
    )(a, w, x, g1, n2, sh2, sc2)


def _top16_rows(vals):
    tm = vals.shape[1]
    rows16 = lax.broadcasted_iota(jnp.int32, (PEER_TOPK, tm), 0)
    top = jnp.zeros((PEER_TOPK, tm), F32)
    for k in range(PEER_TOPK):
        mx = jnp.max(vals, axis=0, keepdims=True)
        top = jnp.where(rows16 == k, mx, top)
        vals = jnp.where(vals == mx, -jnp.inf, vals)
    return top


def _bf16_product(x, y):
    return (x.astype(BF16) * y.astype(BF16)).astype(F32)


def _bf16_twice(x):
    hi = pltpu.bitcast(x.astype(BF16).astype(F32), jnp.uint32)
    return hi | (hi >> 16)


def _rank_pairs(t1, t2, op, fill):
    tm = t1.shape[1]
    row = lax.broadcasted_iota(jnp.int32, (PEER_TOPK, tm), 0)

    def put(slab, val, r0, n):
        return jnp.where(row < r0, slab, jnp.where(row < r0 + n, val, slab))

    def a_with_prefix(slab, a, r0, n):
        t2s = pltpu.roll(t2, r0, 0) if r0 else t2
        return put(slab, op(t1[a:a + 1, :], t2s), r0, n)

    blank = jnp.full((PEER_TOPK, tm), fill, F32)
    slab0 = op(t1[0:1, :], t2)
    slab1 = a_with_prefix(a_with_prefix(a_with_prefix(blank, 1, 0, 8), 2, 8, 5), 4, 13, 3)
    slab2 = blank
    for a, r0, n in ((3, 0, 4), (5, 4, 2), (6, 6, 2), (7, 8, 2)):
        slab2 = a_with_prefix(slab2, a, r0, n)
    tail = op(pltpu.roll(t1, 2, 0), t2[0:1, :])
    slab2 = put(slab2, tail, 10, 6)
    slab3 = put(blank, tail, 0, 2)
    return jnp.concatenate([slab0, slab1, slab2, slab3], axis=0)


def _peer_route_kernel(x_ref, wqt_ref, keys_ref, e1_ref, e2_ref, th_ref, qt_ref):
    qt_ref[...] = lax.dot_general(wqt_ref[...], x_ref[...], (((1,), (1,)), ((), ())),
                                  preferred_element_type=F32).astype(BF16)

    def per_head(h):
        r1 = pl.multiple_of(h * 2 * HALF_KEY, 2 * HALF_KEY)
        r2 = pl.multiple_of(h * 2 * HALF_KEY + HALF_KEY, HALF_KEY)
        s1 = jnp.dot(keys_ref[2 * h], qt_ref[pl.ds(r1, HALF_KEY), :], preferred_element_type=F32)
        s2 = jnp.dot(keys_ref[2 * h + 1], qt_ref[pl.ds(r2, HALF_KEY), :], preferred_element_type=F32)
        t1 = _top16_rows(s1)
        t2 = _top16_rows(s2)
        cand = _rank_pairs(t1, t2, jnp.add, -jnp.inf)
        vals = cand
        tau = None
        for _ in range(PEER_TOPK):
            tau = jnp.max(vals, axis=0, keepdims=True)
            vals = jnp.where(vals == tau, -jnp.inf, vals)
        sel = cand >= tau
        m1 = t1[0:1, :]
        m2 = t2[0:1, :]
        z = jnp.sum(jnp.where(sel, jnp.exp(cand - (m1 + m2)), 0.0), axis=0, keepdims=True)
        rz = 1.0 / z
        e1_ref[h] = _bf16_twice(jnp.exp(s1 - m1) * rz)
        e2_ref[h] = pltpu.bitcast(jnp.exp(s2 - m2).astype(BF16), jnp.uint32)
        prod = _rank_pairs(jnp.exp(t1 - m1) * rz, jnp.exp(t2 - m2), _bf16_product, jnp.inf)
        th_ref[h] = _bf16_twice(jnp.min(jnp.where(sel, prod, jnp.inf), axis=0, keepdims=True))

    def head_pair(hp, carry):
        per_head(2 * hp)
        per_head(2 * hp + 1)
        return carry

    lax.fori_loop(0, PEER_HEADS // 2, head_pair, 0)


def _peer_route(hn, wqt, keys):
    m, d = hn.shape
    tm = 256
    return pl.pallas_call(
        _peer_route_kernel,
        grid=(m // tm,),
        in_specs=[pl.BlockSpec((tm, d), lambda i: (i, 0)), _const_spec((d, d)),
                  _const_spec((2 * PEER_HEADS, N_KEYS, HALF_KEY))],
        out_specs=[pl.BlockSpec((PEER_HEADS, N_KEYS, tm), lambda i: (0, 0, i)),
                   pl.BlockSpec((PEER_HEADS, N_KEYS // 2, tm), lambda i: (0, 0, i)),
                   pl.BlockSpec((PEER_HEADS, 1, tm), lambda i: (0, 0, i))],
        out_shape=[jax.ShapeDtypeStruct((PEER_HEADS, N_KEYS, m), jnp.uint32),
                   jax.ShapeDtypeStruct((PEER_HEADS, N_KEYS // 2, m), jnp.uint32),
                   jax.ShapeDtypeStruct((PEER_HEADS, 1, m), jnp.uint32)],
        scratch_shapes=[pltpu.VMEM((d, tm), BF16)],
        compiler_params=_cparams(("arbitrary",)),
        name="peer_route",
    )(hn, wqt, keys)


def _gelu_erf(x):
    return 0.5 * x * (1.0 + lax.erf(x * math.sqrt(0.5)))


def _peer_expert_kernel(h_ref, u_ref, vt_ref, e1_ref, e2_ref, th_ref, x_ref, g2_ref, *rest, final_norm):
    if final_norm:
        fg_ref, o_ref, acc_ref, s_ref, a_ref = rest
    else:
        o_ref, acc_ref, s_ref, a_ref = rest
    e = pl.program_id(2)
    eb, tm = s_ref.shape
    sb_, tt, d = h_ref.shape
    hb = h_ref[...].reshape(tm, d)

    @pl.when(e == 0)
    def _():
        acc_ref[...] = jnp.zeros_like(acc_ref)

    for sb in range(eb // EXPERT_SUB):
        rows = slice(sb * EXPERT_SUB, (sb + 1) * EXPERT_SUB)
        s_ref[rows, :] = lax.dot_general(u_ref[rows, :], hb, (((1,), (1,)), ((), ())),
                                         preferred_element_type=F32)

    def row_tile(row):
        packed = pltpu.bitcast(jnp.broadcast_to(row, (BF16_ROWS // 2, LANES)), BF16)
        return jnp.concatenate([packed] * (N_KEYS // BF16_ROWS), axis=0)

    zero = jnp.zeros((N_KEYS, LANES), BF16)
    for ii in range(eb // N_KEYS):
        rows = slice(ii * N_KEYS, (ii + 1) * N_KEYS)
        for lt in range(tm // LANES):
            cols = slice(lt * LANES, (lt + 1) * LANES)
            g = zero
            for h in range(PEER_HEADS):
                p = pltpu.bitcast(e2_ref[h, :, cols], BF16) * row_tile(e1_ref[h, ii:ii + 1, cols])
                g = g + jnp.where(p >= row_tile(th_ref[h, :, cols]), p, zero)
            a_ref[rows, cols] = _gelu_erf(s_ref[rows, cols]).astype(BF16) * g
    acc_ref[...] += jnp.dot(vt_ref[0], a_ref[...], preferred_element_type=F32)

    @pl.when(e == pl.num_programs(2) - 1)
    def _():
        y = x_ref[...] + g2_ref[...] * acc_ref[...].T.reshape(sb_, tt, d)
        o_ref[...] = _rmsnorm(y, fg_ref[...]) if final_norm else y


def _peer_experts(hn, u, vt, e1, e2, th, x, g2, final_g):
    b, t, d = hn.shape
    sb, tt = _seq_tiles(b, t)
    tm = sb * tt
    nt = t // tt
    eb = EXPERT_BLOCK
    tile = pl.BlockSpec((sb, tt, d), lambda i, j, e: (i, j, 0))
    in_specs = [tile,
                pl.BlockSpec((eb, d), lambda i, j, e: (e, 0)),
                pl.BlockSpec((1, d, eb), lambda i, j, e: (e, 0, 0)),
                pl.BlockSpec((PEER_HEADS, eb // N_KEYS, tm), lambda i, j, e: (0, e, i * nt + j)),
                pl.BlockSpec((PEER_HEADS, N_KEYS // 2, tm), lambda i, j, e: (0, 0, i * nt + j)),
                pl.BlockSpec((PEER_HEADS, 1, tm), lambda i, j, e: (0, 0, i * nt + j)),
                tile,
                pl.BlockSpec((sb, 1, d), lambda i, j, e: (i, 0, 0))]
    args = [hn, u, vt, e1, e2, th, x, g2]
    if final_g is not None:
        in_specs.append(_const_spec((1, d)))
        args.append(final_g)
    return pl.pallas_call(
        functools.partial(_peer_expert_kernel, final_norm=final_g is not None),
        grid=(b // sb, nt, N_EXPERTS // eb),
        in_specs=in_specs,
        out_specs=tile,
        out_shape=jax.ShapeDtypeStruct((b, t, d), F32),
        scratch_shapes=[pltpu.VMEM((d, tm), F32), pltpu.VMEM((eb, tm), F32), pltpu.VMEM((eb, tm), BF16)],
        compiler_params=_cparams(("arbitrary", "arbitrary", "arbitrary")),
        name="peer_experts",
    )(*args)


def _qkv_kernel(x_ref, g_ref, sh_ref, sc_ref, wdq_ref, gq_ref, wq1_ref, wq2_ref, wdkv_ref, gkv_ref,
                cos_ref, sin_ref, q_ref, ckv_ref, kpe_ref):
    sb, tt, d = x_ref.shape
    hn = _norm_mod(x_ref[...], g_ref[...], sh_ref[...], sc_ref[...]).reshape(sb * tt, d).astype(BF16)
    cq = (_rmsnorm(jnp.dot(hn, wdq_ref[...], preferred_element_type=F32), gq_ref[...]) * Q_SCALE).astype(BF16)
    q1 = jnp.dot(cq, wq1_ref[...], preferred_element_type=F32)
    q2 = jnp.dot(cq, wq2_ref[...], preferred_element_type=F32)
    cos = cos_ref[...][None]
    sin = sin_ref[...][None]
    for h in range(MLA_HEADS):
        q_ref[:, :, h * QK_PAD:h * QK_PAD + NOPE_DIM] = (
            q1[:, h * QK_PAD:h * QK_PAD + NOPE_DIM].reshape(sb, tt, NOPE_DIM).astype(q_ref.dtype))
        pe = (q1[:, h * QK_PAD + NOPE_DIM:(h + 1) * QK_PAD].reshape(sb, tt, LANES) * cos
              + q2[:, h * LANES:(h + 1) * LANES].reshape(sb, tt, LANES) * sin)
        q_ref[:, :, h * QK_PAD + NOPE_DIM:(h + 1) * QK_PAD] = pe.astype(q_ref.dtype)
    kv = jnp.dot(hn, wdkv_ref[...], preferred_element_type=F32)
    ckv_ref[...] = _rmsnorm(kv[:, :KV_LORA], gkv_ref[...]).reshape(sb, tt, KV_LORA)
    kpe = (kv[:, KV_LORA:KV_LORA + LANES].reshape(sb, tt, LANES) * cos
           + kv[:, KV_LORA + LANES:].reshape(sb, tt, LANES) * sin)
    kpe_ref[...] = kpe[:, :, :ROPE_DIM]


def _qkv(x, g, sh, sc, wdq, gq, wq1, wq2, wdkv, gkv, cos, sin):
    b, t, d = x.shape
    sb, tt = _seq_tiles(b, t)
    tt = min(tt, 256)
    seq = pl.BlockSpec((sb, 1, d), lambda i, j: (i, 0, 0))
    rope = pl.BlockSpec((tt, LANES), lambda i, j: (j, 0))
    return pl.pallas_call(
        _qkv_kernel,
        grid=(b // sb, t // tt),
        in_specs=[pl.BlockSpec((sb, tt, d), lambda i, j: (i, j, 0)), _const_spec((1, d)), seq, seq,
                  _const_spec(wdq.shape), _const_spec((1, Q_LORA)), _const_spec(wq1.shape), _const_spec(wq2.shape),
                  _const_spec(wdkv.shape), _const_spec((1, KV_LORA)), rope, rope],
        out_specs=[pl.BlockSpec((sb, tt, MLA_HEADS * QK_PAD), lambda i, j: (i, j, 0)),
                   pl.BlockSpec((sb, tt, KV_LORA), lambda i, j: (i, j, 0)),
                   pl.BlockSpec((sb, tt, ROPE_DIM), lambda i, j: (i, j, 0))],
        out_shape=[jax.ShapeDtypeStruct((b, t, MLA_HEADS * QK_PAD), BF16),
                   jax.ShapeDtypeStruct((b, t, KV_LORA), F32),
                   jax.ShapeDtypeStruct((b, t, ROPE_DIM), F32)],
        compiler_params=_cparams(("arbitrary", "arbitrary")),
        name="l1_qkv",
    )(x, g, sh, sc, wdq, gq, wq1, wq2, wdkv, gkv, cos, sin)


def _kvup_kernel(ckv_ref, kpe_ref, wk_ref, wv_ref, k_ref, v_ref):
    c = ckv_ref[0].astype(BF16)
    kn = jnp.dot(c, wk_ref[...], preferred_element_type=F32)
    vv = jnp.dot(c, wv_ref[...], preferred_element_type=F32)
    kpe = kpe_ref[0].astype(k_ref.dtype)
    tt = kpe.shape[0]
    ones_col = (lax.broadcasted_iota(jnp.int32, (tt, LANES), 1) == 0).astype(v_ref.dtype)
    for h in range(MLA_HEADS):
        k_ref[0, :, h * QK_PAD:h * QK_PAD + NOPE_DIM] = kn[:, h * NOPE_DIM:(h + 1) * NOPE_DIM].astype(k_ref.dtype)
        k_ref[0, :, h * QK_PAD + NOPE_DIM:(h + 1) * QK_PAD] = kpe
        v_ref[0, :, h * V_PAD:h * V_PAD + V_DIM] = vv[:, h * V_DIM:(h + 1) * V_DIM].astype(v_ref.dtype)
        v_ref[0, :, h * V_PAD + V_DIM:(h + 1) * V_PAD] = ones_col


def _kvup(ckv, kpe128, wk, wv):
    b, t, _ = ckv.shape
    tt = max(c for c in range(16, 513, 16) if t % c == 0)
    return pl.pallas_call(
        _kvup_kernel,
        grid=(b, t // tt),
        in_specs=[pl.BlockSpec((1, tt, KV_LORA), lambda i, j: (i, j, 0)),
                  pl.BlockSpec((1, tt, LANES), lambda i, j: (i, j, 0)),
                  _const_spec(wk.shape), _const_spec(wv.shape)],
        out_specs=[pl.BlockSpec((1, tt, MLA_HEADS * QK_PAD), lambda i, j: (i, j, 0)),
                   pl.BlockSpec((1, tt, MLA_HEADS * V_PAD), lambda i, j: (i, j, 0))],
        out_shape=[jax.ShapeDtypeStruct((b, t, MLA_HEADS * QK_PAD), BF16),
                   jax.ShapeDtypeStruct((b, t, MLA_HEADS * V_PAD), BF16)],
        compiler_params=_cparams(("arbitrary", "arbitrary")),
        name="l1_kvup",
    )(ckv, kpe128, wk, wv)


def _qk(q, k):
    return lax.dot_general(q, k, (((1,), (1,)), ((), ())), preferred_element_type=F32)


def _attn_stream_kernel(q_ref, k_ref, v_ref, pat_ref, o_ref, q3, s0, s1, p0, p1, a0, a1, m_run, acc):
    t = q_ref.shape[1]
    qi = pl.program_id(2)
    q3[0] = q_ref[0]
    q3[1] = q_ref[0] + pat_ref[0]
    q3[2] = q_ref[0] + pat_ref[1]

    def scores(k):
        start = pl.multiple_of(jnp.minimum(k, qi) * t, t)
        which = jnp.where(k == qi, 1, jnp.where(k > qi, 2, 0))
        return _qk(q3[which], k_ref[0, pl.ds(start, t), :])

    def values(k):
        start = pl.multiple_of(jnp.clip(k, 0, qi) * t, t)
        return v_ref[0, pl.ds(start, t), :]

    def rescaled(alpha):
        return jnp.concatenate([alpha] * (V_PAD // LANES), axis=1) * acc[...]

    def half(k, s_cur, s_next, p_cur, p_prev, a_cur, a_prev):
        s_next[...] = scores(k + 1)
        acc[...] = rescaled(a_prev[...]) + jnp.dot(p_prev[...], values(k - 1), preferred_element_type=F32)
        m_old = m_run[...]
        m_new = jnp.maximum(m_old, jnp.max(s_cur[...], axis=-1, keepdims=True))
        a_cur[...] = jnp.exp2(m_old - m_new)
        m_run[...] = m_new
        p_cur[...] = jnp.exp2(s_cur[...] - jnp.concatenate([m_new] * (t // LANES), axis=1)).astype(p_cur.dtype)

    s0[...] = scores(0)
    p1[...] = jnp.zeros(p1.shape, p1.dtype)
    a1[...] = jnp.ones(a1.shape, F32)
    acc[...] = jnp.zeros(acc.shape, F32)
    m_run[...] = jnp.full(m_run.shape, NEG_BIG, F32)

    def body(j, carry):
        half(2 * j, s0, s1, p0, p1, a0, a1)
        half(2 * j + 1, s1, s0, p1, p0, a1, a0)
        return carry

    pairs = (qi + 2) // 2
    lax.fori_loop(0, pairs, body, 0)
    out = rescaled(a1[...]) + jnp.dot(p1[...], values(2 * pairs - 1), preferred_element_type=F32)
    o_ref[0] = (out[:, :V_DIM] / out[:, V_DIM:V_DIM + 1]).astype(o_ref.dtype)


def _attn_single_kernel(q_ref, k_ref, v_ref, o_ref, *, q_off):
    tq = q_ref.shape[1]
    tk = k_ref.shape[1]
    s = _qk(q_ref[0], k_ref[0])
    qc = (q_off + lax.broadcasted_iota(jnp.int32, (tq, tk), 0)) // CHUNK
    kc = lax.broadcasted_iota(jnp.int32, (tq, tk), 1) // CHUNK
    s = jnp.where(kc <= qc, s, NEG_BIG)
    p = jnp.exp2(s - jnp.max(s, axis=-1, keepdims=True))
    out = jnp.dot(p.astype(BF16), v_ref[0], preferred_element_type=F32)
    o_ref[0] = (out[:, :V_DIM] / out[:, V_DIM:V_DIM + 1]).astype(o_ref.dtype)


def _attention(q, k, v):
    b, tq_all, _ = q.shape
    tk_all = k.shape[1]
    out_shape = jax.ShapeDtypeStruct((b, tq_all, MLA_HEADS * V_DIM), BF16)
    kv_specs = [pl.BlockSpec((1, tk_all, QK_PAD), lambda i, h, j: (i, 0, h)),
                pl.BlockSpec((1, tk_all, V_PAD), lambda i, h, j: (i, 0, h))]
    if tq_all == tk_all and tq_all % ATTN_TILE == 0:
        t = ATTN_TILE
        r = lax.broadcasted_iota(jnp.int32, (t, QK_PAD), 0) // CHUNK
        j = lax.broadcasted_iota(jnp.int32, (t, QK_PAD), 1) - MASK_COL
        in_cols = (j >= 0) & (j < t // CHUNK)
        pat = jnp.stack([(in_cols & (j > r)), in_cols]).astype(BF16)
        return pl.pallas_call(
            _attn_stream_kernel,
            grid=(b, MLA_HEADS, tq_all // t),
            in_specs=[pl.BlockSpec((1, t, QK_PAD), lambda i, h, j: (i, j, h))] + kv_specs
                     + [_const_spec((2, t, QK_PAD))],
            out_specs=pl.BlockSpec((1, t, V_DIM), lambda i, h, j: (i, j, h)),
            out_shape=out_shape,
            scratch_shapes=[pltpu.VMEM((3, t, QK_PAD), BF16),
                            pltpu.VMEM((t, t), F32), pltpu.VMEM((t, t), F32),
                            pltpu.VMEM((t, t), BF16), pltpu.VMEM((t, t), BF16),
                            pltpu.VMEM((t, LANES), F32), pltpu.VMEM((t, LANES), F32),
                            pltpu.VMEM((t, LANES), F32), pltpu.VMEM((t, V_PAD), F32)],
            compiler_params=_cparams(("arbitrary", "arbitrary", "arbitrary")),
            name="l1_attention",
        )(q, k, v, pat)
    assert tq_all <= ATTN_TILE, "history + long query block is not a shape this trunk step has"
    return pl.pallas_call(
        functools.partial(_attn_single_kernel, q_off=tk_all - tq_all),
        grid=(b, MLA_HEADS, 1),
        in_specs=[pl.BlockSpec((1, tq_all, QK_PAD), lambda i, h, j: (i, 0, h))] + kv_specs,
        out_specs=pl.BlockSpec((1, tq_all, V_DIM), lambda i, h, j: (i, 0, h)),
        out_shape=out_shape,
        compiler_params=_cparams(("arbitrary", "arbitrary", "arbitrary")),
        name="l1_attention_step",
    )(q, k, v)


def _prep_weights(w):
    p = {}
    p["rec_w_in"] = w["rec_w_in"][0].astype(BF16)
    p["rec_w_a"] = w["rec_w_a"][0].astype(BF16)
    p["rec_w_x"] = w["rec_w_x"][0].astype(BF16)
    p["pool_w"] = w["pool_w"][0].astype(BF16)
    p["rec_w_out"] = w["rec_w_out"][0].astype(BF16)
    p["att_w_dq"] = w["att_w_dq"][0].astype(BF16)
    wuq = w["att_w_uq"][0].reshape(Q_LORA, MLA_HEADS, QK_DIM)
    half = ROPE_DIM // 2
    zpad = jnp.zeros((Q_LORA, MLA_HEADS, QK_PAD - QK_DIM), F32)
    p["wq1"] = jnp.concatenate([wuq, zpad], axis=-1).reshape(Q_LORA, MLA_HEADS * QK_PAD).astype(BF16)
    swapped = jnp.concatenate([wuq[..., NOPE_DIM + half:], wuq[..., NOPE_DIM:NOPE_DIM + half],
                               jnp.zeros((Q_LORA, MLA_HEADS, LANES - ROPE_DIM), F32)], axis=-1)
    p["wq2"] = swapped.reshape(Q_LORA, MLA_HEADS * LANES).astype(BF16)
    wdkv = w["att_w_dkv"][0]
    zl = jnp.zeros((D_MODEL, LANES - ROPE_DIM), F32)
    p["wdkv"] = jnp.concatenate([wdkv, zl, wdkv[:, KV_LORA + half:], wdkv[:, KV_LORA:KV_LORA + half], zl],
                                axis=-1).astype(BF16)
    wukv = w["att_w_ukv"][0].reshape(KV_LORA, MLA_HEADS, NOPE_DIM + V_DIM)
    p["wk"] = wukv[..., :NOPE_DIM].reshape(KV_LORA, MLA_HEADS * NOPE_DIM).astype(BF16)
    p["wv"] = wukv[..., NOPE_DIM:].reshape(KV_LORA, MLA_HEADS * V_DIM).astype(BF16)
    p["att_w_out"] = w["att_w_out"][0].astype(BF16)
    p["peer_wqt"] = [w["peer_w_q"][l].T.astype(BF16) for l in range(2)]
    p["peer_keys"] = [w["peer_keys"][l].reshape(2 * PEER_HEADS, N_KEYS, HALF_KEY).astype(BF16) for l in range(2)]
    p["peer_u"] = [w["peer_u"][l].astype(BF16) for l in range(2)]
    p["peer_vt"] = [w["peer_v"][l].astype(BF16).reshape(N_EXPERTS // EXPERT_BLOCK, EXPERT_BLOCK, D_MODEL)
                    .transpose(0, 2, 1) for l in range(2)]
    return p


def _rope_tables(pos):
    half = ROPE_DIM // 2
    inv = ROPE_THETA ** (-(jnp.arange(half, dtype=F32) / half))
    ang = pos.astype(F32)[:, None] * inv[None, :]
    cos = jnp.cos(ang)
    sin = jnp.sin(ang)
    z = jnp.zeros((pos.shape[0], LANES - ROPE_DIM), F32)
    return jnp.concatenate([cos, cos, z], axis=-1), jnp.concatenate([-sin, sin, z], axis=-1)


def _peer(hn, x, g2, p, l, final_g=None):
    b, t, d = hn.shape
    e1, e2, th = _peer_route(hn.reshape(b * t, d), p["peer_wqt"][l], p["peer_keys"][l])
    return _peer_experts(hn, p["peer_u"][l], p["peer_vt"][l], e1, e2, th, x, g2, final_g)


def _trunk(x, mods, pos0, conv_prev, h_prev, pool_prev, ckv_prev, kpe_prev, w, p):
    b, t, d = x.shape
    row = lambda a: a.reshape(1, -1)
    sh1, sc1, g1, sh2, sc2, g2 = mods[0]
    z = _inproj(x, row(w["norm1_g"][0]), sh1, sc1, p["rec_w_in"])
    conv0 = jnp.pad(conv_prev, ((0, 0), (8 - (CONV_WIDTH - 1), 0), (0, 0)))
    pool0 = jnp.pad(pool_prev, ((0, 0), (1, 0), (0, 0)))
    mix, h_last = _recmix(z, conv0, h_prev[:, None, :], pool0, w["rec_conv_w"][0], row(w["rec_conv_b"][0]),
                          p["rec_w_a"], row(w["rec_b_a"][0]), p["rec_w_x"], row(w["rec_b_x"][0]),
                          row(w["rec_lambda"][0]), p["pool_w"], row(w["pool_scale"][0]), pos0)
    new_conv = z[:, t - (CONV_WIDTH - 1):, LRU_WIDTH:2 * LRU_WIDTH]
    new_pool = z[:, t - (POOL_MAX - 1):, 2 * LRU_WIDTH:]
    x, hn = _proj_res(mix, p["rec_w_out"], x, g1, row(w["norm2_g"][0]), sh2, sc2)
    x = _peer(hn, x, g2, p, 0)
    sh1, sc1, g1, sh2, sc2, g2 = mods[1]
    pos = pos0 + jnp.arange(t, dtype=jnp.int32)
    cos, sin = _rope_tables(pos)
    q, ckv, kpe = _qkv(x, row(w["norm1_g"][1]), sh1, sc1, p["att_w_dq"], row(w["att_g_q"][0]), p["wq1"], p["wq2"],
                       p["wdkv"], row(w["att_g_kv"][0]), cos, sin)
    ckv_all = jnp.concatenate([ckv_prev, ckv], axis=1)
    kpe_all = jnp.concatenate([kpe_prev, kpe], axis=1)
    tk = kpe_all.shape[1]
    chunk_in_tile = (jnp.arange(tk, dtype=jnp.int32) % ATTN_TILE) // CHUNK
    mask_cols = jnp.where(chunk_in_tile[:, None] == jnp.arange(LANES - ROPE_DIM, dtype=jnp.int32)[None, :],
                          NEG_BIG, 0.0).astype(F32)
    kpe128 = jnp.concatenate([kpe_all, jnp.broadcast_to(mask_cols[None], (b, tk, LANES - ROPE_DIM))], axis=-1)
    k, v = _kvup(ckv_all, kpe128, p["wk"], p["wv"])
    o = _attention(q, k, v)
    x, hn = _proj_res(o, p["att_w_out"], x, g1, row(w["norm2_g"][1]), sh2, sc2)
    y = _peer(hn, x, g2, p, 1, row(w["final_g"]))
    return y, new_conv[None], h_last.reshape(1, b, LRU_WIDTH), new_pool[None], ckv[None], kpe[None]


def kernel(x_prompt, x_sample, c_prompt, c_sample, state_conv, state_lru_h, state_pool, cache_ckv, cache_kpe,
           ada_w, ada_b, norm1_g, norm2_g, rec_w_in, rec_conv_w, rec_conv_b, rec_w_a, rec_b_a, rec_w_x, rec_b_x,
           rec_lambda, pool_w, pool_scale, rec_w_out, att_w_dq, att_g_q, att_w_uq, att_w_dkv, att_g_kv, att_w_ukv,
           att_w_out, peer_w_q, peer_keys, peer_u, peer_v, final_g):
    w = dict(ada_w=ada_w, ada_b=ada_b, norm1_g=norm1_g, norm2_g=norm2_g, rec_w_in=rec_w_in, rec_conv_w=rec_conv_w,
             rec_conv_b=rec_conv_b, rec_w_a=rec_w_a, rec_b_a=rec_b_a, rec_w_x=rec_w_x, rec_b_x=rec_b_x,
             rec_lambda=rec_lambda, pool_w=pool_w, pool_scale=pool_scale, rec_w_out=rec_w_out, att_w_dq=att_w_dq,
             att_g_q=att_g_q, att_w_uq=att_w_uq, att_w_dkv=att_w_dkv, att_g_kv=att_g_kv, att_w_ukv=att_w_ukv,
             att_w_out=att_w_out, peer_w_q=peer_w_q, peer_keys=peer_keys, peer_u=peer_u, peer_v=peer_v,
             final_g=final_g)
    p = _prep_weights(w)
    bp, tp, d = x_prompt.shape
    bs, ts, _ = x_sample.shape
    past = cache_ckv.shape[2]
    depth = ada_w.shape[0]
    rows = bp + bs
    rows_pad = -(-rows // 8) * 8
    c_all = jnp.pad(jnp.concatenate([c_prompt, c_sample], axis=0), ((0, rows_pad - rows), (0, 0)))
    mod = _ada(c_all, ada_w, ada_b)

    def mods_of(lo, n):
        return [[mod[l, lo:lo + n, k * d:(k + 1) * d][:, None, :] for k in range(6)] for l in range(depth)]

    zeros = lambda *s: jnp.zeros(s, F32)
    out_p = _trunk(x_prompt, mods_of(0, bp), 0, zeros(bp, CONV_WIDTH - 1, LRU_WIDTH), zeros(bp, LRU_WIDTH),
                   zeros(bp, POOL_MAX - 1, POOL_WIDTH), zeros(bp, 0, KV_LORA), zeros(bp, 0, ROPE_DIM), w, p)
    out_s = _trunk(x_sample, mods_of(bp, bs), past, state_conv[0], state_lru_h[0], state_pool[0],
                   cache_ckv[0], cache_kpe[0], w, p)
    return (out_p[0], out_s[0], out_p[1], out_s[1], out_p[2], out_s[2],
            out_p[3], out_s[3], out_p[4], out_s[4], out_p[5], out_s[5])
```

```python
import functools
import math

import jax
import jax.numpy as jnp
from jax import lax
from jax.experimental import pallas as pl
from jax.experimental.pallas import tpu as pltpu

F32 = jnp.float32
BF16 = jnp.bfloat16

D_MODEL = 2048
CHUNK = 64
RMS_EPS = 1e-6
LRU_WIDTH = 1024
LRU_HEADS = 8
LRU_HEAD_DIM = 128
CONV_WIDTH = 4
LRU_C = 8.0
POOL_WIDTH = 1024
POOL_WINDOWS = (2, 4, 8, 16)
POOL_GROUP_DIM = 256
POOL_MAX = 16
MIX_IN = 3072
MLA_HEADS = 16
Q_LORA = 512
KV_LORA = 256
NOPE_DIM = 128
ROPE_DIM = 64
V_DIM = 128
QK_DIM = NOPE_DIM + ROPE_DIM
ROPE_THETA = 10000.0
PEER_HEADS = 8
N_KEYS = 128
N_EXPERTS = N_KEYS * N_KEYS
PEER_TOPK = 16
HALF_KEY = 128

LANES = 128
SUBLANES = 8
CONV_TAIL = SUBLANES
POOL_TAIL = 2 * SUBLANES
BF16_ROWS = 16
EXPERT_BLOCK = 1024
EXPERT_SUB = 256
QK_PAD = 256
V_PAD = 256
MASK_COL = QK_DIM
ATTN_TILE = 512
STEP_HEADS = 4
SOFTMAX_ROWS = 64
Q_SCALE = QK_DIM ** -0.5 * math.log2(math.e)
NEG_BIG = -1e30
VMEM_LIMIT = 56 * 1024 * 1024


def _cparams(sem):
    return pltpu.CompilerParams(dimension_semantics=sem, vmem_limit_bytes=VMEM_LIMIT)


def _const_spec(shape):
    n = len(shape)
    return pl.BlockSpec(shape, lambda *_: (0,) * n)


def _seq_tiles(b, t):
    if t >= 512:
        return 1, 512
    sb = max(1, min(b, 512 // t))
    while b % sb:
        sb -= 1
    return sb, t


def _rmsnorm(x, g):
    return x * lax.rsqrt(jnp.mean(x * x, axis=-1, keepdims=True) + RMS_EPS) * g


def _norm_mod(x, g, sh, sc):
    return _rmsnorm(x, g) * (1.0 + sc) + sh


def _ada_kernel(c_ref, w_ref, b_ref, o_ref):
    c = c_ref[...]
    sc = (c * jax.nn.sigmoid(c)).astype(BF16)
    o_ref[0] = jnp.dot(sc, w_ref[0].astype(BF16), preferred_element_type=F32) + b_ref[0]


def _ada(c_all, ada_w, ada_b):
    depth, d, n = ada_w.shape
    rows = c_all.shape[0]
    tn = 1024
    return pl.pallas_call(
        _ada_kernel,
        grid=(depth, n // tn),
        in_specs=[pl.BlockSpec((rows, d), lambda l, j: (0, 0)),
                  pl.BlockSpec((1, d, tn), lambda l, j: (l, 0, j)),
                  pl.BlockSpec((1, 1, tn), lambda l, j: (l, 0, j))],
        out_specs=pl.BlockSpec((1, rows, tn), lambda l, j: (l, 0, j)),
        out_shape=jax.ShapeDtypeStruct((depth, rows, n), F32),
        compiler_params=_cparams(("arbitrary", "arbitrary")),
        name="ada_mod",
    )(c_all, ada_w, ada_b.reshape(depth, 1, n))


def _inproj_kernel(x_ref, g_ref, sh_ref, sc_ref, w_ref, o_ref):
    sb, tt, d = x_ref.shape
    hn = _norm_mod(x_ref[...], g_ref[...], sh_ref[...], sc_ref[...])
    z = jnp.dot(hn.reshape(sb * tt, d).astype(BF16), w_ref[...], preferred_element_type=F32)
    o_ref[...] = z.reshape(sb, tt, -1)


def _inproj(x, g, sh, sc, w):
    b, t, d = x.shape
    n = w.shape[1]
    sb, tt = _seq_tiles(b, t)
    tt = min(tt, 256)
    seq = pl.BlockSpec((sb, 1, d), lambda i, j: (i, 0, 0))
    return pl.pallas_call(
        _inproj_kernel,
        grid=(b // sb, t // tt),
        in_specs=[pl.BlockSpec((sb, tt, d), lambda i, j: (i, j, 0)),
                  _const_spec((1, d)), seq, seq, _const_spec((d, n))],
        out_specs=pl.BlockSpec((sb, tt, n), lambda i, j: (i, j, 0)),
        out_shape=jax.ShapeDtypeStruct((b, t, n), F32),
        compiler_params=_cparams(("arbitrary", "arbitrary")),
        name="l0_inproj",
    )(x, g, sh, sc, w)


def _shift_rows(x, d, fill):
    rolled = pltpu.roll(x, d, 0)
    rows = lax.broadcasted_iota(jnp.int32, x.shape, 0)
    return jnp.where(rows >= d, rolled, fill)


def _gelu_tanh(x):
    return 0.5 * x * (1.0 + jnp.tanh(math.sqrt(2.0 / math.pi) * (x + 0.044715 * (x * x * x))))


def _recmix_kernel(z_ref, conv0_ref, h0_ref, pool0_ref, cw_ref, cb_ref, wa_ref, ba_ref, wx_ref, bx_ref,
                   lam_ref, pw_ref, ps_ref, mix_ref, hlast_ref, ctail, ptail, hstate, ext, *, pos0):
    sb, tt, _ = z_ref.shape
    j = pl.program_id(1)

    @pl.when(j == 0)
    def _():
        ctail[...] = conv0_ref[...]
        ptail[...] = pool0_ref[...]
        hstate[...] = h0_ref[...]

    lam = lam_ref[...]
    neg = -lam
    softplus_neg_lam = jnp.maximum(neg, 0.0) + jnp.log1p(jnp.exp(-jnp.abs(neg)))
    pos = pos0 + j * tt + lax.broadcasted_iota(jnp.int32, (tt, 1), 0)

    def per_seq(s, carry):
        rec = z_ref[s, :, LRU_WIDTH:2 * LRU_WIDTH]
        ext[0:CONV_TAIL, :] = ctail[s]
        ext[CONV_TAIL:CONV_TAIL + tt, :] = rec
        first = CONV_TAIL - (CONV_WIDTH - 1)
        xc = cb_ref[...] + sum(ext[first + k:first + k + tt, :] * cw_ref[k:k + 1, :] for k in range(CONV_WIDTH))
        ctail[s] = ext[tt:tt + CONV_TAIL, :]
        xcb = xc.astype(BF16)
        r_parts, i_parts = [], []
        for h in range(LRU_HEADS):
            sl = slice(h * LRU_HEAD_DIM, (h + 1) * LRU_HEAD_DIM)
            r_parts.append(jnp.dot(xcb[:, sl], wa_ref[h], preferred_element_type=F32))
            i_parts.append(jnp.dot(xcb[:, sl], wx_ref[h], preferred_element_type=F32))
        r = jax.nn.sigmoid(jnp.concatenate(r_parts, axis=-1) + ba_ref[...])
        i = jax.nn.sigmoid(jnp.concatenate(i_parts, axis=-1) + bx_ref[...])
        log_a = -LRU_C * r * softplus_neg_lam
        a = jnp.exp(log_a)
        th = jnp.tanh(log_a)
        mult = jnp.sqrt(-2.0 * th / (1.0 - th))
        mult = jnp.where(pos == 0, 1.0, mult)
        u = mult * (i * xc)
        acc_a, acc_b = a, u
        d = 1
        while d < tt:
            a_sh = _shift_rows(acc_a, d, 1.0)
            b_sh = _shift_rows(acc_b, d, 0.0)
            acc_b = acc_a * b_sh + acc_b
            acc_a = acc_a * a_sh
            d *= 2
        hseq = acc_a * hstate[s] + acc_b
        hstate[s] = hseq[tt - 1:tt, :]
        gate = z_ref[s, :, 0:LRU_WIDTH]
        mix_ref[s, :, 0:LRU_WIDTH] = (_gelu_tanh(gate) * hseq).astype(mix_ref.dtype)
        pin = z_ref[s, :, 2 * LRU_WIDTH:]
        ext[0:POOL_TAIL, :] = ptail[s]
        ext[POOL_TAIL:POOL_TAIL + tt, :] = pin
        ptail[s] = ext[tt:tt + POOL_TAIL, :]
        for g, w in enumerate(POOL_WINDOWS):
            sl = slice(g * POOL_GROUP_DIM, (g + 1) * POOL_GROUP_DIM)
            win = sum(ext[POOL_TAIL - k:POOL_TAIL - k + tt, sl] for k in range(w))
            cnt = jnp.minimum(pos + 1, w).astype(F32)
            pooled = win / cnt - pin[:, sl]
            bo = jnp.dot(pooled.astype(BF16), pw_ref[g], preferred_element_type=F32) * ps_ref[:, sl]
            mix_ref[s, :, LRU_WIDTH + g * POOL_GROUP_DIM:LRU_WIDTH + (g + 1) * POOL_GROUP_DIM] = bo.astype(mix_ref.dtype)
        return carry

    lax.fori_loop(0, sb, per_seq, 0)

    @pl.when(j == pl.num_programs(1) - 1)
    def _():
        hlast_ref[...] = hstate[...]


def _recmix(z, conv0, h0, pool0, cw, cb, wa, ba, wx, bx, lam, pw, ps, pos0):
    b, t, _ = z.shape
    sb, tt = _seq_tiles(b, t)
    tt = min(tt, 256)
    c = LRU_WIDTH
    row = _const_spec((1, c))
    return pl.pallas_call(
        functools.partial(_recmix_kernel, pos0=pos0),
        grid=(b // sb, t // tt),
        in_specs=[pl.BlockSpec((sb, tt, MIX_IN), lambda i, j: (i, j, 0)),
                  pl.BlockSpec((sb, CONV_TAIL, c), lambda i, j: (i, 0, 0)),
                  pl.BlockSpec((sb, 1, c), lambda i, j: (i, 0, 0)),
                  pl.BlockSpec((sb, POOL_TAIL, c), lambda i, j: (i, 0, 0)),
                  _const_spec((CONV_WIDTH, c)), row,
                  _const_spec((LRU_HEADS, LRU_HEAD_DIM, LRU_HEAD_DIM)), row,
                  _const_spec((LRU_HEADS, LRU_HEAD_DIM, LRU_HEAD_DIM)), row, row,
                  _const_spec((len(POOL_WINDOWS), POOL_GROUP_DIM, POOL_GROUP_DIM)), row],
        out_specs=[pl.BlockSpec((sb, tt, 2 * c), lambda i, j: (i, j, 0)),
                   pl.BlockSpec((sb, 1, c), lambda i, j: (i, 0, 0))],
        out_shape=[jax.ShapeDtypeStruct((b, t, 2 * c), BF16),
                   jax.ShapeDtypeStruct((b, 1, c), F32)],
        scratch_shapes=[pltpu.VMEM((sb, CONV_TAIL, c), F32), pltpu.VMEM((sb, POOL_TAIL, c), F32),
                        pltpu.VMEM((sb, 1, c), F32), pltpu.VMEM((tt + POOL_TAIL, c), F32)],
        compiler_params=_cparams(("arbitrary", "arbitrary")),
        name="l0_recmix",
    )(z, conv0, h0, pool0, cw, cb, wa, ba, wx, bx, lam, pw, ps)


def _proj_res_kernel(a_ref, w_ref, x_ref, g1_ref, n2_ref, sh_ref, sc_ref, xo_ref, ho_ref):
    sb, tt, k = a_ref.shape
    mix = jnp.dot(a_ref[...].reshape(sb * tt, k), w_ref[...], preferred_element_type=F32)
    xn = x_ref[...] + g1_ref[...] * mix.reshape(sb, tt, -1)
    xo_ref[...] = xn
    ho_ref[...] = _norm_mod(xn, n2_ref[...], sh_ref[...], sc_ref[...]).astype(ho_ref.dtype)


def _proj_res(a, w, x, g1, n2, sh2, sc2):
    b, t, d = x.shape
    k = a.shape[-1]
    sb, tt = _seq_tiles(b, t)
    seq = pl.BlockSpec((sb, 1, d), lambda i, j: (i, 0, 0))
    tile = pl.BlockSpec((sb, tt, d), lambda i, j: (i, j, 0))
    return pl.pallas_call(
        _proj_res_kernel,
        grid=(b // sb, t // tt),
        in_specs=[pl.BlockSpec((sb, tt, k), lambda i, j: (i, j, 0)), _const_spec((k, d)), tile,
                  seq, _const_spec((1, d)), seq, seq],
        out_specs=[tile, tile],
        out_shape=[jax.ShapeDtypeStruct((b, t, d), F32), jax.ShapeDtypeStruct((b, t, d), BF16)],
        compiler_params=_cparams(("arbitrary", "arbitrary")),
        name="proj_res_norm",
    )(a, w, x, g1, n2, sh2, sc2)


def _top16_rows(vals):
    tm = vals.shape[1]
    rows16 = lax.broadcasted_iota(jnp.int32, (PEER_TOPK, tm), 0)
    top = jnp.zeros((PEER_TOPK, tm), F32)
    for k in range(PEER_TOPK):
        mx = jnp.max(vals, axis=0, keepdims=True)
        top = jnp.where(rows16 == k, mx, top)
        vals = jnp.where(vals == mx, -jnp.inf, vals)
    return top


def _bf16_product(x, y):
    return (x.astype(BF16) * y.astype(BF16)).astype(F32)


def _bf16_twice(x):
    hi = pltpu.bitcast(x.astype(BF16).astype(F32), jnp.uint32)
    return hi | (hi >> 16)


def _rank_pairs(t1, t2, op, fill):
    tm = t1.shape[1]
    row = lax.broadcasted_iota(jnp.int32, (PEER_TOPK, tm), 0)

    def put(slab, val, r0, n):
        return jnp.where(row < r0, slab, jnp.where(row < r0 + n, val, slab))

    def a_with_prefix(slab, a, r0, n):
        t2s = pltpu.roll(t2, r0, 0) if r0 else t2
        return put(slab, op(t1[a:a + 1, :], t2s), r0, n)

    blank = jnp.full((PEER_TOPK, tm), fill, F32)
    slab0 = op(t1[0:1, :], t2)
    slab1 = a_with_prefix(a_with_prefix(a_with_prefix(blank, 1, 0, 8), 2, 8, 5), 4, 13, 3)
    slab2 = blank
    for a, r0, n in ((3, 0, 4), (5, 4, 2), (6, 6, 2), (7, 8, 2)):
        slab2 = a_with_prefix(slab2, a, r0, n)
    tail = op(pltpu.roll(t1, 2, 0), t2[0:1, :])
    slab2 = put(slab2, tail, 10, 6)
    slab3 = put(blank, tail, 0, 2)
    return jnp.concatenate([slab0, slab1, slab2, slab3], axis=0)


def _peer_route_kernel(x_ref, wqt_ref, keys_ref, e1_ref, e2_ref, th_ref, qt_ref):
    qt_ref[...] = lax.dot_general(wqt_ref[...], x_ref[...], (((1,), (1,)), ((), ())),
                                  preferred_element_type=F32).astype(BF16)

    def per_head(h):
        r1 = pl.multiple_of(h * 2 * HALF_KEY, 2 * HALF_KEY)
        r2 = pl.multiple_of(h * 2 * HALF_KEY + HALF_KEY, HALF_KEY)
        s1 = jnp.dot(keys_ref[2 * h], qt_ref[pl.ds(r1, HALF_KEY), :], preferred_element_type=F32)
        s2 = jnp.dot(keys_ref[2 * h + 1], qt_ref[pl.ds(r2, HALF_KEY), :], preferred_element_type=F32)
        t1 = _top16_rows(s1)
        t2 = _top16_rows(s2)
        cand = _rank_pairs(t1, t2, jnp.add, -jnp.inf)
        vals = cand
        tau = None
        for _ in range(PEER_TOPK):
            tau = jnp.max(vals, axis=0, keepdims=True)
            vals = jnp.where(vals == tau, -jnp.inf, vals)
        sel = cand >= tau
        m1 = t1[0:1, :]
        m2 = t2[0:1, :]
        z = jnp.sum(jnp.where(sel, jnp.exp(cand - (m1 + m2)), 0.0), axis=0, keepdims=True)
        rz = 1.0 / z
        e1_ref[h] = _bf16_twice(jnp.exp(s1 - m1) * rz)
        e2_ref[h] = pltpu.bitcast(jnp.exp(s2 - m2).astype(BF16), jnp.uint32)
        prod = _rank_pairs(jnp.exp(t1 - m1) * rz, jnp.exp(t2 - m2), _bf16_product, jnp.inf)
        th_ref[h] = _bf16_twice(jnp.min(jnp.where(sel, prod, jnp.inf), axis=0, keepdims=True))

    def head_pair(hp, carry):
        per_head(2 * hp)
        per_head(2 * hp + 1)
        return carry

    lax.fori_loop(0, PEER_HEADS // 2, head_pair, 0)


def _peer_route(hn, wqt, keys):
    m, d = hn.shape
    tm = 256
    return pl.pallas_call(
        _peer_route_kernel,
        grid=(m // tm,),
        in_specs=[pl.BlockSpec((tm, d), lambda i: (i, 0)), _const_spec((d, d)),
                  _const_spec((2 * PEER_HEADS, N_KEYS, HALF_KEY))],
        out_specs=[pl.BlockSpec((PEER_HEADS, N_KEYS, tm), lambda i: (0, 0, i)),
                   pl.BlockSpec((PEER_HEADS, N_KEYS // 2, tm), lambda i: (0, 0, i)),
                   pl.BlockSpec((PEER_HEADS, 1, tm), lambda i: (0, 0, i))],
        out_shape=[jax.ShapeDtypeStruct((PEER_HEADS, N_KEYS, m), jnp.uint32),
                   jax.ShapeDtypeStruct((PEER_HEADS, N_KEYS // 2, m), jnp.uint32),
                   jax.ShapeDtypeStruct((PEER_HEADS, 1, m), jnp.uint32)],
        scratch_shapes=[pltpu.VMEM((d, tm), BF16)],
        compiler_params=_cparams(("arbitrary",)),
        name="peer_route",
    )(hn, wqt, keys)


def _gelu_erf(x):
    return 0.5 * x * (1.0 + lax.erf(x * math.sqrt(0.5)))


def _peer_expert_kernel(h_ref, u_ref, vt_ref, e1_ref, e2_ref, th_ref, x_ref, g2_ref, *rest, final_norm):
    if final_norm:
        fg_ref, o_ref, acc_ref, s_ref, a_ref = rest
    else:
        o_ref, acc_ref, s_ref, a_ref = rest
    e = pl.program_id(2)
    eb, tm = s_ref.shape
    sb_, tt, d = h_ref.shape
    hb = h_ref[...].reshape(tm, d)

    @pl.when(e == 0)
    def _():
        acc_ref[...] = jnp.zeros_like(acc_ref)

    for sb in range(eb // EXPERT_SUB):
        rows = slice(sb * EXPERT_SUB, (sb + 1) * EXPERT_SUB)
        s_ref[rows, :] = lax.dot_general(u_ref[rows, :], hb, (((1,), (1,)), ((), ())),
                                         preferred_element_type=F32)

    def row_tile(row):
        packed = pltpu.bitcast(jnp.broadcast_to(row, (BF16_ROWS // 2, LANES)), BF16)
        return jnp.concatenate([packed] * (N_KEYS // BF16_ROWS), axis=0)

    zero = jnp.zeros((N_KEYS, LANES), BF16)
    for ii in range(eb // N_KEYS):
        rows = slice(ii * N_KEYS, (ii + 1) * N_KEYS)
        for lt in range(tm // LANES):
            cols = slice(lt * LANES, (lt + 1) * LANES)
            g = zero
            for h in range(PEER_HEADS):
                p = pltpu.bitcast(e2_ref[h, :, cols], BF16) * row_tile(e1_ref[h, ii:ii + 1, cols])
                g = g + jnp.where(p >= row_tile(th_ref[h, :, cols]), p, zero)
            a_ref[rows, cols] = _gelu_erf(s_ref[rows, cols]).astype(BF16) * g
    acc_ref[...] += jnp.dot(vt_ref[0], a_ref[...], preferred_element_type=F32)

    @pl.when(e == pl.num_programs(2) - 1)
    def _():
        y = x_ref[...] + g2_ref[...] * acc_ref[...].T.reshape(sb_, tt, d)
        o_ref[...] = _rmsnorm(y, fg_ref[...]) if final_norm else y


def _peer_experts(hn, u, vt, e1, e2, th, x, g2, final_g):
    b, t, d = hn.shape
    sb, tt = _seq_tiles(b, t)
    tm = sb * tt
    nt = t // tt
    eb = EXPERT_BLOCK
    tile = pl.BlockSpec((sb, tt, d), lambda i, j, e: (i, j, 0))
    in_specs = [tile,
                pl.BlockSpec((eb, d), lambda i, j, e: (e, 0)),
                pl.BlockSpec((1, d, eb), lambda i, j, e: (e, 0, 0)),
                pl.BlockSpec((PEER_HEADS, eb // N_KEYS, tm), lambda i, j, e: (0, e, i * nt + j)),
                pl.BlockSpec((PEER_HEADS, N_KEYS // 2, tm), lambda i, j, e: (0, 0, i * nt + j)),
                pl.BlockSpec((PEER_HEADS, 1, tm), lambda i, j, e: (0, 0, i * nt + j)),
                tile,
                pl.BlockSpec((sb, 1, d), lambda i, j, e: (i, 0, 0))]
    args = [hn, u, vt, e1, e2, th, x, g2]
    if final_g is not None:
        in_specs.append(_const_spec((1, d)))
        args.append(final_g)
    return pl.pallas_call(
        functools.partial(_peer_expert_kernel, final_norm=final_g is not None),
        grid=(b // sb, nt, N_EXPERTS // eb),
        in_specs=in_specs,
        out_specs=tile,
        out_shape=jax.ShapeDtypeStruct((b, t, d), F32),
        scratch_shapes=[pltpu.VMEM((d, tm), F32), pltpu.VMEM((eb, tm), F32), pltpu.VMEM((eb, tm), BF16)],
        compiler_params=_cparams(("arbitrary", "arbitrary", "arbitrary")),
        name="peer_experts",
    )(*args)


def _qkv_kernel(x_ref, g_ref, sh_ref, sc_ref, wdq_ref, gq_ref, wq1_ref, wq2_ref, wdkv_ref, gkv_ref,
                cos_ref, sin_ref, q_ref, ckv_ref, kpe_ref):
    sb, tt, d = x_ref.shape
    hn = _norm_mod(x_ref[...], g_ref[...], sh_ref[...], sc_ref[...]).reshape(sb * tt, d).astype(BF16)
    cq = (_rmsnorm(jnp.dot(hn, wdq_ref[...], preferred_element_type=F32), gq_ref[...]) * Q_SCALE).astype(BF16)
    q1 = jnp.dot(cq, wq1_ref[...], preferred_element_type=F32)
    q2 = jnp.dot(cq, wq2_ref[...], preferred_element_type=F32)
    cos = cos_ref[...][None]
    sin = sin_ref[...][None]
    for h in range(MLA_HEADS):
        q_ref[:, :, h * QK_PAD:h * QK_PAD + NOPE_DIM] = (
            q1[:, h * QK_PAD:h * QK_PAD + NOPE_DIM].reshape(sb, tt, NOPE_DIM).astype(q_ref.dtype))
        pe = (q1[:, h * QK_PAD + NOPE_DIM:(h + 1) * QK_PAD].reshape(sb, tt, LANES) * cos
              + q2[:, h * LANES:(h + 1) * LANES].reshape(sb, tt, LANES) * sin)
        q_ref[:, :, h * QK_PAD + NOPE_DIM:(h + 1) * QK_PAD] = pe.astype(q_ref.dtype)
    kv = jnp.dot(hn, wdkv_ref[...], preferred_element_type=F32)
    ckv_ref[...] = _rmsnorm(kv[:, :KV_LORA], gkv_ref[...]).reshape(sb, tt, KV_LORA)
    kpe = (kv[:, KV_LORA:KV_LORA + LANES].reshape(sb, tt, LANES) * cos
           + kv[:, KV_LORA + LANES:].reshape(sb, tt, LANES) * sin)
    kpe_ref[...] = kpe[:, :, :ROPE_DIM]


def _qkv(x, g, sh, sc, wdq, gq, wq1, wq2, wdkv, gkv, cos, sin):
    b, t, d = x.shape
    sb, tt = _seq_tiles(b, t)
    tt = min(tt, 256)
    seq = pl.BlockSpec((sb, 1, d), lambda i, j: (i, 0, 0))
    rope = pl.BlockSpec((tt, LANES), lambda i, j: (j, 0))
    return pl.pallas_call(
        _qkv_kernel,
        grid=(b // sb, t // tt),
        in_specs=[pl.BlockSpec((sb, tt, d), lambda i, j: (i, j, 0)), _const_spec((1, d)), seq, seq,
                  _const_spec(wdq.shape), _const_spec((1, Q_LORA)), _const_spec(wq1.shape), _const_spec(wq2.shape),
                  _const_spec(wdkv.shape), _const_spec((1, KV_LORA)), rope, rope],
        out_specs=[pl.BlockSpec((sb, tt, MLA_HEADS * QK_PAD), lambda i, j: (i, j, 0)),
                   pl.BlockSpec((sb, tt, KV_LORA), lambda i, j: (i, j, 0)),
                   pl.BlockSpec((sb, tt, ROPE_DIM), lambda i, j: (i, j, 0))],
        out_shape=[jax.ShapeDtypeStruct((b, t, MLA_HEADS * QK_PAD), BF16),
                   jax.ShapeDtypeStruct((b, t, KV_LORA), F32),
                   jax.ShapeDtypeStruct((b, t, ROPE_DIM), F32)],
        compiler_params=_cparams(("arbitrary", "arbitrary")),
        name="l1_qkv",
    )(x, g, sh, sc, wdq, gq, wq1, wq2, wdkv, gkv, cos, sin)


def _kvup_kernel(ckv_ref, kpe_ref, wk_ref, wv_ref, k_ref, v_ref):
    c = ckv_ref[0].astype(BF16)
    kn = jnp.dot(c, wk_ref[...], preferred_element_type=F32)
    vv = jnp.dot(c, wv_ref[...], preferred_element_type=F32)
    kpe = kpe_ref[0].astype(k_ref.dtype)
    tt = kpe.shape[0]
    ones_col = (lax.broadcasted_iota(jnp.int32, (tt, LANES), 1) == 0).astype(v_ref.dtype)
    for h in range(MLA_HEADS):
        k_ref[0, :, h * QK_PAD:h * QK_PAD + NOPE_DIM] = kn[:, h * NOPE_DIM:(h + 1) * NOPE_DIM].astype(k_ref.dtype)
        k_ref[0, :, h * QK_PAD + NOPE_DIM:(h + 1) * QK_PAD] = kpe
        v_ref[0, :, h * V_PAD:h * V_PAD + V_DIM] = vv[:, h * V_DIM:(h + 1) * V_DIM].astype(v_ref.dtype)
        v_ref[0, :, h * V_PAD + V_DIM:(h + 1) * V_PAD] = ones_col


def _kvup(ckv, kpe128, wk, wv):
    b, t, _ = ckv.shape
    tt = max(c for c in range(16, 513, 16) if t % c == 0)
    return pl.pallas_call(
        _kvup_kernel,
        grid=(b, t // tt),
        in_specs=[pl.BlockSpec((1, tt, KV_LORA), lambda i, j: (i, j, 0)),
                  pl.BlockSpec((1, tt, LANES), lambda i, j: (i, j, 0)),
                  _const_spec(wk.shape), _const_spec(wv.shape)],
        out_specs=[pl.BlockSpec((1, tt, MLA_HEADS * QK_PAD), lambda i, j: (i, j, 0)),
                   pl.BlockSpec((1, tt, MLA_HEADS * V_PAD), lambda i, j: (i, j, 0))],
        out_shape=[jax.ShapeDtypeStruct((b, t, MLA_HEADS * QK_PAD), BF16),
                   jax.ShapeDtypeStruct((b, t, MLA_HEADS * V_PAD), BF16)],
        compiler_params=_cparams(("arbitrary", "arbitrary")),
        name="l1_kvup",
    )(ckv, kpe128, wk, wv)


def _qk(q, k):
    return lax.dot_general(q, k, (((1,), (1,)), ((), ())), preferred_element_type=F32)


def _attn_stream_kernel(q_ref, k_ref, v_ref, pat_ref, o_ref, q3, s0, s1, p0, p1, a0, a1, m_run, acc):
    t = q_ref.shape[1]
    qi = pl.program_id(2)
    q3[0] = q_ref[0]
    q3[1] = q_ref[0] + pat_ref[0]
    q3[2] = q_ref[0] + pat_ref[1]

    def scores(k):
        start = pl.multiple_of(jnp.minimum(k, qi) * t, t)
        which = jnp.where(k == qi, 1, jnp.where(k > qi, 2, 0))
        return _qk(q3[which], k_ref[0, pl.ds(start, t), :])

    def values(k):
        start = pl.multiple_of(jnp.clip(k, 0, qi) * t, t)
        return v_ref[0, pl.ds(start, t), :]

    def rescaled(alpha):
        return jnp.concatenate([alpha] * (V_PAD // LANES), axis=1) * acc[...]

    def half(k, s_cur, s_next, p_cur, p_prev, a_cur, a_prev):
        s_next[...] = scores(k + 1)
        acc[...] = rescaled(a_prev[...]) + jnp.dot(p_prev[...], values(k - 1), preferred_element_type=F32)
        m_old = m_run[...]
        m_new = jnp.maximum(m_old, jnp.max(s_cur[...], axis=-1, keepdims=True))
        a_cur[...] = jnp.exp2(m_old - m_new)
        m_run[...] = m_new
        p_cur[...] = jnp.exp2(s_cur[...] - jnp.concatenate([m_new] * (t // LANES), axis=1)).astype(p_cur.dtype)

    s0[...] = scores(0)
    p1[...] = jnp.zeros(p1.shape, p1.dtype)
    a1[...] = jnp.ones(a1.shape, F32)
    acc[...] = jnp.zeros(acc.shape, F32)
    m_run[...] = jnp.full(m_run.shape, NEG_BIG, F32)

    def body(j, carry):
        half(2 * j, s0, s1, p0, p1, a0, a1)
        half(2 * j + 1, s1, s0, p1, p0, a1, a0)
        return carry

    pairs = (qi + 2) // 2
    lax.fori_loop(0, pairs, body, 0)
    out = rescaled(a1[...]) + jnp.dot(p1[...], values(2 * pairs - 1), preferred_element_type=F32)
    o_ref[0] = (out[:, :V_DIM] / out[:, V_DIM:V_DIM + 1]).astype(o_ref.dtype)


def _attn_single_kernel(q_ref, k_ref, v_ref, o_ref, *, q_off):
    tq = q_ref.shape[1]
    tk = k_ref.shape[1]
    qc = (q_off + lax.broadcasted_iota(jnp.int32, (tq, tk), 0)) // CHUNK
    kc = lax.broadcasted_iota(jnp.int32, (tq, tk), 1) // CHUNK
    visible = kc <= qc
    for h in range(STEP_HEADS):
        qk_cols = slice(h * QK_PAD, (h + 1) * QK_PAD)
        s = _qk(q_ref[0, :, qk_cols], k_ref[0, :, qk_cols])
        s = jnp.where(visible, s, NEG_BIG)
        p = jnp.exp2(s - jnp.max(s, axis=-1, keepdims=True))
        out = jnp.dot(p.astype(BF16), v_ref[0, :, h * V_PAD:(h + 1) * V_PAD], preferred_element_type=F32)
        o_ref[0, :, h * V_DIM:(h + 1) * V_DIM] = (out[:, :V_DIM] / out[:, V_DIM:V_DIM + 1]).astype(o_ref.dtype)


def _attention(q, k, v):
    b, tq_all, _ = q.shape
    tk_all = k.shape[1]
    out_shape = jax.ShapeDtypeStruct((b, tq_all, MLA_HEADS * V_DIM), BF16)
    kv_specs = [pl.BlockSpec((1, tk_all, QK_PAD), lambda i, h, j: (i, 0, h)),
                pl.BlockSpec((1, tk_all, V_PAD), lambda i, h, j: (i, 0, h))]
    if tq_all == tk_all and tq_all % ATTN_TILE == 0:
        t = ATTN_TILE
        r = lax.broadcasted_iota(jnp.int32, (t, QK_PAD), 0) // CHUNK
        j = lax.broadcasted_iota(jnp.int32, (t, QK_PAD), 1) - MASK_COL
        in_cols = (j >= 0) & (j < t // CHUNK)
        pat = jnp.stack([(in_cols & (j > r)), in_cols]).astype(BF16)
        return pl.pallas_call(
            _attn_stream_kernel,
            grid=(b, MLA_HEADS, tq_all // t),
            in_specs=[pl.BlockSpec((1, t, QK_PAD), lambda i, h, j: (i, j, h))] + kv_specs
                     + [_const_spec((2, t, QK_PAD))],
            out_specs=pl.BlockSpec((1, t, V_DIM), lambda i, h, j: (i, j, h)),
            out_shape=out_shape,
            scratch_shapes=[pltpu.VMEM((3, t, QK_PAD), BF16),
                            pltpu.VMEM((t, t), F32), pltpu.VMEM((t, t), F32),
                            pltpu.VMEM((t, t), BF16), pltpu.VMEM((t, t), BF16),
                            pltpu.VMEM((t, LANES), F32), pltpu.VMEM((t, LANES), F32),
                            pltpu.VMEM((t, LANES), F32), pltpu.VMEM((t, V_PAD), F32)],
            compiler_params=_cparams(("arbitrary", "arbitrary", "arbitrary")),
            name="l1_attention",
        )(q, k, v, pat)
    assert tq_all <= ATTN_TILE, "history + long query block is not a shape this trunk step has"
    hg = STEP_HEADS
    return pl.pallas_call(
        functools.partial(_attn_single_kernel, q_off=tk_all - tq_all),
        grid=(b, MLA_HEADS // hg),
        in_specs=[pl.BlockSpec((1, tq_all, hg * QK_PAD), lambda i, h: (i, 0, h)),
                  pl.BlockSpec((1, tk_all, hg * QK_PAD), lambda i, h: (i, 0, h)),
                  pl.BlockSpec((1, tk_all, hg * V_PAD), lambda i, h: (i, 0, h))],
        out_specs=pl.BlockSpec((1, tq_all, hg * V_DIM), lambda i, h: (i, 0, h)),
        out_shape=out_shape,
        compiler_params=_cparams(("arbitrary", "arbitrary")),
        name="l1_attention_step",
    )(q, k, v)


def _prep_weights(w):
    p = {}
    p["rec_w_in"] = w["rec_w_in"][0].astype(BF16)
    p["rec_w_a"] = w["rec_w_a"][0].astype(BF16)
    p["rec_w_x"] = w["rec_w_x"][0].astype(BF16)
    p["pool_w"] = w["pool_w"][0].astype(BF16)
    p["rec_w_out"] = w["rec_w_out"][0].astype(BF16)
    p["att_w_dq"] = w["att_w_dq"][0].astype(BF16)
    wuq = w["att_w_uq"][0].reshape(Q_LORA, MLA_HEADS, QK_DIM)
    half = ROPE_DIM // 2
    zpad = jnp.zeros((Q_LORA, MLA_HEADS, QK_PAD - QK_DIM), F32)
    p["wq1"] = jnp.concatenate([wuq, zpad], axis=-1).reshape(Q_LORA, MLA_HEADS * QK_PAD).astype(BF16)
    swapped = jnp.concatenate([wuq[..., NOPE_DIM + half:], wuq[..., NOPE_DIM:NOPE_DIM + half],
                               jnp.zeros((Q_LORA, MLA_HEADS, LANES - ROPE_DIM), F32)], axis=-1)
    p["wq2"] = swapped.reshape(Q_LORA, MLA_HEADS * LANES).astype(BF16)
    wdkv = w["att_w_dkv"][0]
    zl = jnp.zeros((D_MODEL, LANES - ROPE_DIM), F32)
    p["wdkv"] = jnp.concatenate([wdkv, zl, wdkv[:, KV_LORA + half:], wdkv[:, KV_LORA:KV_LORA + half], zl],
                                axis=-1).astype(BF16)
    wukv = w["att_w_ukv"][0].reshape(KV_LORA, MLA_HEADS, NOPE_DIM + V_DIM)
    p["wk"] = wukv[..., :NOPE_DIM].reshape(KV_LORA, MLA_HEADS * NOPE_DIM).astype(BF16)
    p["wv"] = wukv[..., NOPE_DIM:].reshape(KV_LORA, MLA_HEADS * V_DIM).astype(BF16)
    p["att_w_out"] = w["att_w_out"][0].astype(BF16)
    p["peer_wqt"] = [w["peer_w_q"][l].T.astype(BF16) for l in range(2)]
    p["peer_keys"] = [w["peer_keys"][l].reshape(2 * PEER_HEADS, N_KEYS, HALF_KEY).astype(BF16) for l in range(2)]
    p["peer_u"] = [w["peer_u"][l].astype(BF16) for l in range(2)]
    p["peer_vt"] = [w["peer_v"][l].astype(BF16).reshape(N_EXPERTS // EXPERT_BLOCK, EXPERT_BLOCK, D_MODEL)
                    .transpose(0, 2, 1) for l in range(2)]
    return p


def _rope_tables(pos):
    half = ROPE_DIM // 2
    inv = ROPE_THETA ** (-(jnp.arange(half, dtype=F32) / half))
    ang = pos.astype(F32)[:, None] * inv[None, :]
    cos = jnp.cos(ang)
    sin = jnp.sin(ang)
    z = jnp.zeros((pos.shape[0], LANES - ROPE_DIM), F32)
    return jnp.concatenate([cos, cos, z], axis=-1), jnp.concatenate([-sin, sin, z], axis=-1)


def _peer(hn, x, g2, p, l, final_g=None):
    b, t, d = hn.shape
    e1, e2, th = _peer_route(hn.reshape(b * t, d), p["peer_wqt"][l], p["peer_keys"][l])
    return _peer_experts(hn, p["peer_u"][l], p["peer_vt"][l], e1, e2, th, x, g2, final_g)


def _trunk(x, mods, pos0, conv_prev, h_prev, pool_prev, ckv_prev, kpe_prev, w, p):
    b, t, d = x.shape
    row = lambda a: a.reshape(1, -1)
    sh1, sc1, g1, sh2, sc2, g2 = mods[0]
    z = _inproj(x, row(w["norm1_g"][0]), sh1, sc1, p["rec_w_in"])
    conv0 = jnp.pad(conv_prev, ((0, 0), (CONV_TAIL - (CONV_WIDTH - 1), 0), (0, 0)))
    pool0 = jnp.pad(pool_prev, ((0, 0), (POOL_TAIL - (POOL_MAX - 1), 0), (0, 0)))
    mix, h_last = _recmix(z, conv0, h_prev[:, None, :], pool0, w["rec_conv_w"][0], row(w["rec_conv_b"][0]),
                          p["rec_w_a"], row(w["rec_b_a"][0]), p["rec_w_x"], row(w["rec_b_x"][0]),
                          row(w["rec_lambda"][0]), p["pool_w"], row(w["pool_scale"][0]), pos0)
    new_conv = z[:, t - (CONV_WIDTH - 1):, LRU_WIDTH:2 * LRU_WIDTH]
    new_pool = z[:, t - (POOL_MAX - 1):, 2 * LRU_WIDTH:]
    x, hn = _proj_res(mix, p["rec_w_out"], x, g1, row(w["norm2_g"][0]), sh2, sc2)
    x = _peer(hn, x, g2, p, 0)
    sh1, sc1, g1, sh2, sc2, g2 = mods[1]
    pos = pos0 + jnp.arange(t, dtype=jnp.int32)
    cos, sin = _rope_tables(pos)
    q, ckv, kpe = _qkv(x, row(w["norm1_g"][1]), sh1, sc1, p["att_w_dq"], row(w["att_g_q"][0]), p["wq1"], p["wq2"],
                       p["wdkv"], row(w["att_g_kv"][0]), cos, sin)
    ckv_all = jnp.concatenate([ckv_prev, ckv], axis=1)
    kpe_all = jnp.concatenate([kpe_prev, kpe], axis=1)
    tk = kpe_all.shape[1]
    chunk_in_tile = (jnp.arange(tk, dtype=jnp.int32) % ATTN_TILE) // CHUNK
    mask_cols = jnp.where(chunk_in_tile[:, None] == jnp.arange(LANES - ROPE_DIM, dtype=jnp.int32)[None, :],
                          NEG_BIG, 0.0).astype(F32)
    kpe128 = jnp.concatenate([kpe_all, jnp.broadcast_to(mask_cols[None], (b, tk, LANES - ROPE_DIM))], axis=-1)
    k, v = _kvup(ckv_all, kpe128, p["wk"], p["wv"])
    o = _attention(q, k, v)
    x, hn = _proj_res(o, p["att_w_out"], x, g1, row(w["norm2_g"][1]), sh2, sc2)
    y = _peer(hn, x, g2, p, 1, row(w["final_g"]))
    return y, new_conv[None], h_last.reshape(1, b, LRU_WIDTH), new_pool[None], ckv[None], kpe[None]


def kernel(x_prompt, x_sample, c_prompt, c_sample, state_conv, state_lru_h, state_pool, cache_ckv, cache_kpe,
           ada_w, ada_b, norm1_g, norm2_g, rec_w_in, rec_conv_w, rec_conv_b, rec_w_a, rec_b_a, rec_w_x, rec_b_x,
           rec_lambda, pool_w, pool_scale, rec_w_out, att_w_dq, att_g_q, att_w_uq, att_w_dkv, att_g_kv, att_w_ukv,
           att_w_out, peer_w_q, peer_keys, peer_u, peer_v, final_g):
    w = dict(ada_w=ada_w, ada_b=ada_b, norm1_g=norm1_g, norm2_g=norm2_g, rec_w_in=rec_w_in, rec_conv_w=rec_conv_w,
             rec_conv_b=rec_conv_b, rec_w_a=rec_w_a, rec_b_a=rec_b_a, rec_w_x=rec_w_x, rec_b_x=rec_b_x,
             rec_lambda=rec_lambda, pool_w=pool_w, pool_scale=pool_scale, rec_w_out=rec_w_out, att_w_dq=att_w_dq,
             att_g_q=att_g_q, att_w_uq=att_w_uq, att_w_dkv=att_w_dkv, att_g_kv=att_g_kv, att_w_ukv=att_w_ukv,
             att_w_out=att_w_out, peer_w_q=peer_w_q, peer_keys=peer_keys, peer_u=peer_u, peer_v=peer_v,
             final_g=final_g)
    p = _prep_weights(w)
    bp, tp, d = x_prompt.shape
    bs, ts, _ = x_sample.shape
    past = cache_ckv.shape[2]
    depth = ada_w.shape[0]
    rows = bp + bs
    rows_pad = -(-rows // 8) * 8
    c_all = jnp.pad(jnp.concatenate([c_prompt, c_sample], axis=0), ((0, rows_pad - rows), (0, 0)))
    mod = _ada(c_all, ada_w, ada_b)

    def mods_of(lo, n):
        return [[mod[l, lo:lo + n, k * d:(k + 1) * d][:, None, :] for k in range(6)] for l in range(depth)]

    zeros = lambda *s: jnp.zeros(s, F32)
    out_p = _trunk(x_prompt, mods_of(0, bp), 0, zeros(bp, CONV_WIDTH - 1, LRU_WIDTH), zeros(bp, LRU_WIDTH),
                   zeros(bp, POOL_MAX - 1, POOL_WIDTH), zeros(bp, 0, KV_LORA), zeros(bp, 0, ROPE_DIM), w, p)
    out_s = _trunk(x_sample, mods_of(bp, bs), past, state_conv[0], state_lru_h[0], state_pool[0],
                   cache_ckv[0], cache_kpe[0], w, p)
    return (out_p[0], out_s[0], out_p[1], out_s[1], out_p[2], out_s[2],
            out_p[3], out_s[3], out_p[4], out_s[4], out_p[5], out_s[5])
```

```python
import functools
import math

import jax
import jax.numpy as jnp
from jax import lax
from jax.experimental import pallas as pl
from jax.experimental.pallas import tpu as pltpu

F32 = jnp.float32
BF16 = jnp.bfloat16

D_MODEL = 2048
CHUNK = 64
RMS_EPS = 1e-6
LRU_WIDTH = 1024
LRU_HEADS = 8
LRU_HEAD_DIM = 128
CONV_WIDTH = 4
LRU_C = 8.0
POOL_WIDTH = 1024
POOL_WINDOWS = (2, 4, 8, 16)
POOL_GROUP_DIM = 256
POOL_MAX = 16
MIX_IN = 3072
MLA_HEADS = 16
Q_LORA = 512
KV_LORA = 256
NOPE_DIM = 128
ROPE_DIM = 64
V_DIM = 128
QK_DIM = NOPE_DIM + ROPE_DIM
ROPE_THETA = 10000.0
PEER_HEADS = 8
N_KEYS = 128
N_EXPERTS = N_KEYS * N_KEYS
PEER_TOPK = 16
HALF_KEY = 128

LANES = 128
SUBLANES = 8
CONV_TAIL = SUBLANES
POOL_TAIL = 2 * SUBLANES
BF16_ROWS = 16
EXPERT_BLOCK = 1024
EXPERT_SUB = 256
QK_PAD = 256
V_PAD = 256
MASK_COL = QK_DIM
ATTN_TILE = 512
STEP_HEADS = 4
SOFTMAX_ROWS = 64
Q_SCALE = QK_DIM ** -0.5 * math.log2(math.e)
NEG_BIG = -1e30
VMEM_LIMIT = 56 * 1024 * 1024


def _cparams(sem):
    return pltpu.CompilerParams(dimension_semantics=sem, vmem_limit_bytes=VMEM_LIMIT)


def _const_spec(shape):
    n = len(shape)
    return pl.BlockSpec(shape, lambda *_: (0,) * n)


def _seq_tiles(b, t):
    if t >= 512:
        return 1, 512
    sb = max(1, min(b, 512 // t))
    while b % sb:
        sb -= 1
    return sb, t


def _rmsnorm(x, g):
    return x * lax.rsqrt(jnp.mean(x * x, axis=-1, keepdims=True) + RMS_EPS) * g


def _norm_mod(x, g, sh, sc):
    return _rmsnorm(x, g) * (1.0 + sc) + sh


def _ada_kernel(c_ref, w_ref, b_ref, o_ref):
    c = c_ref[...]
    sc = (c * jax.nn.sigmoid(c)).astype(BF16)
    o_ref[0] = jnp.dot(sc, w_ref[0].astype(BF16), preferred_element_type=F32) + b_ref[0]


def _ada(c_all, ada_w, ada_b):
    depth, d, n = ada_w.shape
    rows = c_all.shape[0]
    tn = 1024
    return pl.pallas_call(
        _ada_kernel,
        grid=(depth, n // tn),
        in_specs=[pl.BlockSpec((rows, d), lambda l, j: (0, 0)),
                  pl.BlockSpec((1, d, tn), lambda l, j: (l, 0, j)),
                  pl.BlockSpec((1, 1, tn), lambda l, j: (l, 0, j))],
        out_specs=pl.BlockSpec((1, rows, tn), lambda l, j: (l, 0, j)),
        out_shape=jax.ShapeDtypeStruct((depth, rows, n), F32),
        compiler_params=_cparams(("arbitrary", "arbitrary")),
        name="ada_mod",
    )(c_all, ada_w, ada_b.reshape(depth, 1, n))


def _inproj_kernel(x_ref, g_ref, sh_ref, sc_ref, w_ref, o_ref):
    sb, tt, d = x_ref.shape
    hn = _norm_mod(x_ref[...], g_ref[...], sh_ref[...], sc_ref[...])
    z = jnp.dot(hn.reshape(sb * tt, d).astype(BF16), w_ref[...], preferred_element_type=F32)
    o_ref[...] = z.reshape(sb, tt, -1)


def _inproj(x, g, sh, sc, w):
    b, t, d = x.shape
    n = w.shape[1]
    sb, tt = _seq_tiles(b, t)
    tt = min(tt, 256)
    seq = pl.BlockSpec((sb, 1, d), lambda i, j: (i, 0, 0))
    return pl.pallas_call(
        _inproj_kernel,
        grid=(b // sb, t // tt),
        in_specs=[pl.BlockSpec((sb, tt, d), lambda i, j: (i, j, 0)),
                  _const_spec((1, d)), seq, seq, _const_spec((d, n))],
        out_specs=pl.BlockSpec((sb, tt, n), lambda i, j: (i, j, 0)),
        out_shape=jax.ShapeDtypeStruct((b, t, n), F32),
        compiler_params=_cparams(("arbitrary", "arbitrary")),
        name="l0_inproj",
    )(x, g, sh, sc, w)


def _shift_rows(x, d, fill):
    rolled = pltpu.roll(x, d, 0)
    rows = lax.broadcasted_iota(jnp.int32, x.shape, 0)
    return jnp.where(rows >= d, rolled, fill)


def _gelu_tanh(x):
    return 0.5 * x * (1.0 + jnp.tanh(math.sqrt(2.0 / math.pi) * (x + 0.044715 * (x * x * x))))


def _recmix_kernel(z_ref, conv0_ref, h0_ref, pool0_ref, cw_ref, cb_ref, wa_ref, ba_ref, wx_ref, bx_ref,
                   lam_ref, pw_ref, ps_ref, mix_ref, hlast_ref, ctail, ptail, hstate, ext, *, pos0):
    sb, tt, _ = z_ref.shape
    j = pl.program_id(1)

    @pl.when(j == 0)
    def _():
        ctail[...] = conv0_ref[...]
        ptail[...] = pool0_ref[...]
        hstate[...] = h0_ref[...]

    lam = lam_ref[...]
    neg = -lam
    softplus_neg_lam = jnp.maximum(neg, 0.0) + jnp.log1p(jnp.exp(-jnp.abs(neg)))
    pos = pos0 + j * tt + lax.broadcasted_iota(jnp.int32, (tt, 1), 0)

    def per_seq(s, carry):
        rec = z_ref[s, :, LRU_WIDTH:2 * LRU_WIDTH]
        ext[0:CONV_TAIL, :] = ctail[s]
        ext[CONV_TAIL:CONV_TAIL + tt, :] = rec
        first = CONV_TAIL - (CONV_WIDTH - 1)
        xc = cb_ref[...] + sum(ext[first + k:first + k + tt, :] * cw_ref[k:k + 1, :] for k in range(CONV_WIDTH))
        ctail[s] = ext[tt:tt + CONV_TAIL, :]
        xcb = xc.astype(BF16)
        r_parts, i_parts = [], []
        for h in range(LRU_HEADS):
            sl = slice(h * LRU_HEAD_DIM, (h + 1) * LRU_HEAD_DIM)
            r_parts.append(jnp.dot(xcb[:, sl], wa_ref[h], preferred_element_type=F32))
            i_parts.append(jnp.dot(xcb[:, sl], wx_ref[h], preferred_element_type=F32))
        r = jax.nn.sigmoid(jnp.concatenate(r_parts, axis=-1) + ba_ref[...])
        i = jax.nn.sigmoid(jnp.concatenate(i_parts, axis=-1) + bx_ref[...])
        log_a = -LRU_C * r * softplus_neg_lam
        a = jnp.exp(log_a)
        th = jnp.tanh(log_a)
        mult = jnp.sqrt(-2.0 * th / (1.0 - th))
        mult = jnp.where(pos == 0, 1.0, mult)
        u = mult * (i * xc)
        acc_a, acc_b = a, u
        d = 1
        while d < tt:
            a_sh = _shift_rows(acc_a, d, 1.0)
            b_sh = _shift_rows(acc_b, d, 0.0)
            acc_b = acc_a * b_sh + acc_b
            acc_a = acc_a * a_sh
            d *= 2
        hseq = acc_a * hstate[s] + acc_b
        hstate[s] = hseq[tt - 1:tt, :]
        gate = z_ref[s, :, 0:LRU_WIDTH]
        mix_ref[s, :, 0:LRU_WIDTH] = (_gelu_tanh(gate) * hseq).astype(mix_ref.dtype)
        pin = z_ref[s, :, 2 * LRU_WIDTH:]
        ext[0:POOL_TAIL, :] = ptail[s]
        ext[POOL_TAIL:POOL_TAIL + tt, :] = pin
        ptail[s] = ext[tt:tt + POOL_TAIL, :]
        for g, w in enumerate(POOL_WINDOWS):
            sl = slice(g * POOL_GROUP_DIM, (g + 1) * POOL_GROUP_DIM)
            win = sum(ext[POOL_TAIL - k:POOL_TAIL - k + tt, sl] for k in range(w))
            cnt = jnp.minimum(pos + 1, w).astype(F32)
            pooled = win / cnt - pin[:, sl]
            bo = jnp.dot(pooled.astype(BF16), pw_ref[g], preferred_element_type=F32) * ps_ref[:, sl]
            mix_ref[s, :, LRU_WIDTH + g * POOL_GROUP_DIM:LRU_WIDTH + (g + 1) * POOL_GROUP_DIM] = bo.astype(mix_ref.dtype)
        return carry

    lax.fori_loop(0, sb, per_seq, 0)

    @pl.when(j == pl.num_programs(1) - 1)
    def _():
        hlast_ref[...] = hstate[...]


def _recmix(z, conv0, h0, pool0, cw, cb, wa, ba, wx, bx, lam, pw, ps, pos0):
    b, t, _ = z.shape
    sb, tt = _seq_tiles(b, t)
    tt = min(tt, 256)
    c = LRU_WIDTH
    row = _const_spec((1, c))
    return pl.pallas_call(
        functools.partial(_recmix_kernel, pos0=pos0),
        grid=(b // sb, t // tt),
        in_specs=[pl.BlockSpec((sb, tt, MIX_IN), lambda i, j: (i, j, 0)),
                  pl.BlockSpec((sb, CONV_TAIL, c), lambda i, j: (i, 0, 0)),
                  pl.BlockSpec((sb, 1, c), lambda i, j: (i, 0, 0)),
                  pl.BlockSpec((sb, POOL_TAIL, c), lambda i, j: (i, 0, 0)),
                  _const_spec((CONV_WIDTH, c)), row,
                  _const_spec((LRU_HEADS, LRU_HEAD_DIM, LRU_HEAD_DIM)), row,
                  _const_spec((LRU_HEADS, LRU_HEAD_DIM, LRU_HEAD_DIM)), row, row,
                  _const_spec((len(POOL_WINDOWS), POOL_GROUP_DIM, POOL_GROUP_DIM)), row],
        out_specs=[pl.BlockSpec((sb, tt, 2 * c), lambda i, j: (i, j, 0)),
                   pl.BlockSpec((sb, 1, c), lambda i, j: (i, 0, 0))],
        out_shape=[jax.ShapeDtypeStruct((b, t, 2 * c), BF16),
                   jax.ShapeDtypeStruct((b, 1, c), F32)],
        scratch_shapes=[pltpu.VMEM((sb, CONV_TAIL, c), F32), pltpu.VMEM((sb, POOL_TAIL, c), F32),
                        pltpu.VMEM((sb, 1, c), F32), pltpu.VMEM((tt + POOL_TAIL, c), F32)],
        compiler_params=_cparams(("arbitrary", "arbitrary")),
        name="l0_recmix",
    )(z, conv0, h0, pool0, cw, cb, wa, ba, wx, bx, lam, pw, ps)


def _proj_res_kernel(a_ref, w_ref, x_ref, g1_ref, n2_ref, sh_ref, sc_ref, xo_ref, ho_ref):
    sb, tt, k = a_ref.shape
    mix = jnp.dot(a_ref[...].reshape(sb * tt, k), w_ref[...], preferred_element_type=F32)
    xn = x_ref[...] + g1_ref[...] * mix.reshape(sb, tt, -1)
    xo_ref[...] = xn
    ho_ref[...] = _norm_mod(xn, n2_ref[...], sh_ref[...], sc_ref[...]).astype(ho_ref.dtype)


def _proj_res(a, w, x, g1, n2, sh2, sc2):
    b, t, d = x.shape
    k = a.shape[-1]
    sb, tt = _seq_tiles(b, t)
    seq = pl.BlockSpec((sb, 1, d), lambda i, j: (i, 0, 0))
    tile = pl.BlockSpec((sb, tt, d), lambda i, j: (i, j, 0))
    return pl.pallas_call(
        _proj_res_kernel,
        grid=(b // sb, t // tt),
        in_specs=[pl.BlockSpec((sb, tt, k), lambda i, j: (i, j, 0)), _const_spec((k, d)), tile,
                  seq, _const_spec((1, d)), seq, seq],
        out_specs=[tile, tile],
        out_shape=[jax.ShapeDtypeStruct((b, t, d), F32), jax.ShapeDtypeStruct((b, t, d), BF16)],
        compiler_params=_cparams(("arbitrary", "arbitrary")),
        name="proj_res_norm",
    )(a, w, x, g1, n2, sh2, sc2)


def _top16_rows(vals):
    tm = vals.shape[1]
    rows16 = lax.broadcasted_iota(jnp.int32, (PEER_TOPK, tm), 0)
    top = jnp.zeros((PEER_TOPK, tm), F32)
    for k in range(PEER_TOPK):
        mx = jnp.max(vals, axis=0, keepdims=True)
        top = jnp.where(rows16 == k, mx, top)
        vals = jnp.where(vals == mx, -jnp.inf, vals)
    return top


def _bf16_product(x, y):
    return (x.astype(BF16) * y.astype(BF16)).astype(F32)


def _bf16_twice(x):
    hi = pltpu.bitcast(x.astype(BF16).astype(F32), jnp.uint32)
    return hi | (hi >> 16)


def _rank_pairs(t1, t2, op, fill):
    tm = t1.shape[1]
    row = lax.broadcasted_iota(jnp.int32, (PEER_TOPK, tm), 0)

    def put(slab, val, r0, n):
        return jnp.where(row < r0, slab, jnp.where(row < r0 + n, val, slab))

    def a_with_prefix(slab, a, r0, n):
        t2s = pltpu.roll(t2, r0, 0) if r0 else t2
        return put(slab, op(t1[a:a + 1, :], t2s), r0, n)

    blank = jnp.full((PEER_TOPK, tm), fill, F32)
    slab0 = op(t1[0:1, :], t2)
    slab1 = a_with_prefix(a_with_prefix(a_with_prefix(blank, 1, 0, 8), 2, 8, 5), 4, 13, 3)
    slab2 = blank
    for a, r0, n in ((3, 0, 4), (5, 4, 2), (6, 6, 2), (7, 8, 2)):
        slab2 = a_with_prefix(slab2, a, r0, n)
    tail = op(pltpu.roll(t1, 2, 0), t2[0:1, :])
    slab2 = put(slab2, tail, 10, 6)
    slab3 = put(blank, tail, 0, 2)
    return jnp.concatenate([slab0, slab1, slab2, slab3], axis=0)


def _peer_route_kernel(x_ref, wqt_ref, keys_ref, e1_ref, e2_ref, th_ref, qt_ref):
    qt_ref[...] = lax.dot_general(wqt_ref[...], x_ref[...], (((1,), (1,)), ((), ())),
                                  preferred_element_type=F32).astype(BF16)

    def per_head(h):
        r1 = pl.multiple_of(h * 2 * HALF_KEY, 2 * HALF_KEY)
        r2 = pl.multiple_of(h * 2 * HALF_KEY + HALF_KEY, HALF_KEY)
        s1 = jnp.dot(keys_ref[2 * h], qt_ref[pl.ds(r1, HALF_KEY), :], preferred_element_type=F32)
        s2 = jnp.dot(keys_ref[2 * h + 1], qt_ref[pl.ds(r2, HALF_KEY), :], preferred_element_type=F32)
        t1 = _top16_rows(s1)
        t2 = _top16_rows(s2)
        cand = _rank_pairs(t1, t2, jnp.add, -jnp.inf)
        vals = cand
        tau = None
        for _ in range(PEER_TOPK):
            tau = jnp.max(vals, axis=0, keepdims=True)
            vals = jnp.where(vals == tau, -jnp.inf, vals)
        sel = cand >= tau
        m1 = t1[0:1, :]
        m2 = t2[0:1, :]
        z = jnp.sum(jnp.where(sel, jnp.exp(cand - (m1 + m2)), 0.0), axis=0, keepdims=True)
        rz = 1.0 / z
        e1_ref[h] = _bf16_twice(jnp.exp(s1 - m1) * rz)
        e2_ref[h] = pltpu.bitcast(jnp.exp(s2 - m2).astype(BF16), jnp.uint32)
        prod = _rank_pairs(jnp.exp(t1 - m1) * rz, jnp.exp(t2 - m2), _bf16_product, jnp.inf)
        th_ref[h] = _bf16_twice(jnp.min(jnp.where(sel, prod, jnp.inf), axis=0, keepdims=True))

    def head_pair(hp, carry):
        per_head(2 * hp)
        per_head(2 * hp + 1)
        return carry

    lax.fori_loop(0, PEER_HEADS // 2, head_pair, 0)


def _peer_route(hn, wqt, keys):
    m, d = hn.shape
    tm = 256
    return pl.pallas_call(
        _peer_route_kernel,
        grid=(m // tm,),
        in_specs=[pl.BlockSpec((tm, d), lambda i: (i, 0)), _const_spec((d, d)),
                  _const_spec((2 * PEER_HEADS, N_KEYS, HALF_KEY))],
        out_specs=[pl.BlockSpec((PEER_HEADS, N_KEYS, tm), lambda i: (0, 0, i)),
                   pl.BlockSpec((PEER_HEADS, N_KEYS // 2, tm), lambda i: (0, 0, i)),
                   pl.BlockSpec((PEER_HEADS, 1, tm), lambda i: (0, 0, i))],
        out_shape=[jax.ShapeDtypeStruct((PEER_HEADS, N_KEYS, m), jnp.uint32),
                   jax.ShapeDtypeStruct((PEER_HEADS, N_KEYS // 2, m), jnp.uint32),
                   jax.ShapeDtypeStruct((PEER_HEADS, 1, m), jnp.uint32)],
        scratch_shapes=[pltpu.VMEM((d, tm), BF16)],
        compiler_params=_cparams(("arbitrary",)),
        name="peer_route",
    )(hn, wqt, keys)


def _gelu_erf(x):
    return 0.5 * x * (1.0 + lax.erf(x * math.sqrt(0.5)))


def _peer_expert_kernel(h_ref, u_ref, vt_ref, e1_ref, e2_ref, th_ref, x_ref, g2_ref, *rest, final_norm):
    if final_norm:
        fg_ref, o_ref, acc_ref, s_ref, a_ref = rest
    else:
        o_ref, acc_ref, s_ref, a_ref = rest
    e = pl.program_id(2)
    eb, tm = s_ref.shape
    sb_, tt, d = h_ref.shape
    hb = h_ref[...].reshape(tm, d)

    @pl.when(e == 0)
    def _():
        acc_ref[...] = jnp.zeros_like(acc_ref)

    for sb in range(eb // EXPERT_SUB):
        rows = slice(sb * EXPERT_SUB, (sb + 1) * EXPERT_SUB)
        s_ref[rows, :] = lax.dot_general(u_ref[rows, :], hb, (((1,), (1,)), ((), ())),
                                         preferred_element_type=F32)

    def row_tile(row):
        packed = pltpu.bitcast(jnp.broadcast_to(row, (BF16_ROWS // 2, LANES)), BF16)
        return jnp.concatenate([packed] * (N_KEYS // BF16_ROWS), axis=0)

    zero = jnp.zeros((N_KEYS, LANES), BF16)
    for ii in range(eb // N_KEYS):
        rows = slice(ii * N_KEYS, (ii + 1) * N_KEYS)
        for lt in range(tm // LANES):
            cols = slice(lt * LANES, (lt + 1) * LANES)
            g = zero
            for h in range(PEER_HEADS):
                p = pltpu.bitcast(e2_ref[h, :, cols], BF16) * row_tile(e1_ref[h, ii:ii + 1, cols])
                g = g + jnp.where(p >= row_tile(th_ref[h, :, cols]), p, zero)
            a_ref[rows, cols] = _gelu_erf(s_ref[rows, cols]).astype(BF16) * g
    acc_ref[...] += jnp.dot(vt_ref[0], a_ref[...], preferred_element_type=F32)

    @pl.when(e == pl.num_programs(2) - 1)
    def _():
        y = x_ref[...] + g2_ref[...] * acc_ref[...].T.reshape(sb_, tt, d)
        o_ref[...] = _rmsnorm(y, fg_ref[...]) if final_norm else y


def _peer_experts(hn, u, vt, e1, e2, th, x, g2, final_g):
    b, t, d = hn.shape
    sb, tt = _seq_tiles(b, t)
    tm = sb * tt
    nt = t // tt
    eb = EXPERT_BLOCK
    tile = pl.BlockSpec((sb, tt, d), lambda i, j, e: (i, j, 0))
    in_specs = [tile,
                pl.BlockSpec((eb, d), lambda i, j, e: (e, 0)),
                pl.BlockSpec((1, d, eb), lambda i, j, e: (e, 0, 0)),
                pl.BlockSpec((PEER_HEADS, eb // N_KEYS, tm), lambda i, j, e: (0, e, i * nt + j)),
                pl.BlockSpec((PEER_HEADS, N_KEYS // 2, tm), lambda i, j, e: (0, 0, i * nt + j)),
                pl.BlockSpec((PEER_HEADS, 1, tm), lambda i, j, e: (0, 0, i * nt + j)),
                tile,
                pl.BlockSpec((sb, 1, d), lambda i, j, e: (i, 0, 0))]
    args = [hn, u, vt, e1, e2, th, x, g2]
    if final_g is not None:
        in_specs.append(_const_spec((1, d)))
        args.append(final_g)
    return pl.pallas_call(
        functools.partial(_peer_expert_kernel, final_norm=final_g is not None),
        grid=(b // sb, nt, N_EXPERTS // eb),
        in_specs=in_specs,
        out_specs=tile,
        out_shape=jax.ShapeDtypeStruct((b, t, d), F32),
        scratch_shapes=[pltpu.VMEM((d, tm), F32), pltpu.VMEM((eb, tm), F32), pltpu.VMEM((eb, tm), BF16)],
        compiler_params=_cparams(("arbitrary", "arbitrary", "arbitrary")),
        name="peer_experts",
    )(*args)


def _qkv_kernel(x_ref, g_ref, sh_ref, sc_ref, wdq_ref, gq_ref, wq1_ref, wq2_ref, wdkv_ref, gkv_ref,
                cos_ref, sin_ref, q_ref, ckv_ref, kpe_ref):
    sb, tt, d = x_ref.shape
    hn = _norm_mod(x_ref[...], g_ref[...], sh_ref[...], sc_ref[...]).reshape(sb * tt, d).astype(BF16)
    cq = (_rmsnorm(jnp.dot(hn, wdq_ref[...], preferred_element_type=F32), gq_ref[...]) * Q_SCALE).astype(BF16)
    q1 = jnp.dot(cq, wq1_ref[...], preferred_element_type=F32)
    q2 = jnp.dot(cq, wq2_ref[...], preferred_element_type=F32)
    cos = cos_ref[...][None]
    sin = sin_ref[...][None]
    for h in range(MLA_HEADS):
        q_ref[:, :, h * QK_PAD:h * QK_PAD + NOPE_DIM] = (
            q1[:, h * QK_PAD:h * QK_PAD + NOPE_DIM].reshape(sb, tt, NOPE_DIM).astype(q_ref.dtype))
        pe = (q1[:, h * QK_PAD + NOPE_DIM:(h + 1) * QK_PAD].reshape(sb, tt, LANES) * cos
              + q2[:, h * LANES:(h + 1) * LANES].reshape(sb, tt, LANES) * sin)
        q_ref[:, :, h * QK_PAD + NOPE_DIM:(h + 1) * QK_PAD] = pe.astype(q_ref.dtype)
    kv = jnp.dot(hn, wdkv_ref[...], preferred_element_type=F32)
    ckv_ref[...] = _rmsnorm(kv[:, :KV_LORA], gkv_ref[...]).reshape(sb, tt, KV_LORA)
    kpe = (kv[:, KV_LORA:KV_LORA + LANES].reshape(sb, tt, LANES) * cos
           + kv[:, KV_LORA + LANES:].reshape(sb, tt, LANES) * sin)
    kpe_ref[...] = kpe[:, :, :ROPE_DIM]


def _qkv(x, g, sh, sc, wdq, gq, wq1, wq2, wdkv, gkv, cos, sin):
    b, t, d = x.shape
    sb, tt = _seq_tiles(b, t)
    tt = min(tt, 256)
    seq = pl.BlockSpec((sb, 1, d), lambda i, j: (i, 0, 0))
    rope = pl.BlockSpec((tt, LANES), lambda i, j: (j, 0))
    return pl.pallas_call(
        _qkv_kernel,
        grid=(b // sb, t // tt),
        in_specs=[pl.BlockSpec((sb, tt, d), lambda i, j: (i, j, 0)), _const_spec((1, d)), seq, seq,
                  _const_spec(wdq.shape), _const_spec((1, Q_LORA)), _const_spec(wq1.shape), _const_spec(wq2.shape),
                  _const_spec(wdkv.shape), _const_spec((1, KV_LORA)), rope, rope],
        out_specs=[pl.BlockSpec((sb, tt, MLA_HEADS * QK_PAD), lambda i, j: (i, j, 0)),
                   pl.BlockSpec((sb, tt, KV_LORA), lambda i, j: (i, j, 0)),
                   pl.BlockSpec((sb, tt, ROPE_DIM), lambda i, j: (i, j, 0))],
        out_shape=[jax.ShapeDtypeStruct((b, t, MLA_HEADS * QK_PAD), BF16),
                   jax.ShapeDtypeStruct((b, t, KV_LORA), F32),
                   jax.ShapeDtypeStruct((b, t, ROPE_DIM), F32)],
        compiler_params=_cparams(("arbitrary", "arbitrary")),
        name="l1_qkv",
    )(x, g, sh, sc, wdq, gq, wq1, wq2, wdkv, gkv, cos, sin)


def _kvup_kernel(ckv_ref, kpe_ref, wk_ref, wv_ref, k_ref, v_ref):
    c = ckv_ref[0].astype(BF16)
    kn = jnp.dot(c, wk_ref[...], preferred_element_type=F32)
    vv = jnp.dot(c, wv_ref[...], preferred_element_type=F32)
    kpe = kpe_ref[0].astype(k_ref.dtype)
    tt = kpe.shape[0]
    ones_col = (lax.broadcasted_iota(jnp.int32, (tt, LANES), 1) == 0).astype(v_ref.dtype)
    for h in range(MLA_HEADS):
        k_ref[0, :, h * QK_PAD:h * QK_PAD + NOPE_DIM] = kn[:, h * NOPE_DIM:(h + 1) * NOPE_DIM].astype(k_ref.dtype)
        k_ref[0, :, h * QK_PAD + NOPE_DIM:(h + 1) * QK_PAD] = kpe
        v_ref[0, :, h * V_PAD:h * V_PAD + V_DIM] = vv[:, h * V_DIM:(h + 1) * V_DIM].astype(v_ref.dtype)
        v_ref[0, :, h * V_PAD + V_DIM:(h + 1) * V_PAD] = ones_col


def _kvup(ckv, kpe128, wk, wv):
    b, t, _ = ckv.shape
    tt = max(c for c in range(16, 513, 16) if t % c == 0)
    return pl.pallas_call(
        _kvup_kernel,
        grid=(b, t // tt),
        in_specs=[pl.BlockSpec((1, tt, KV_LORA), lambda i, j: (i, j, 0)),
                  pl.BlockSpec((1, tt, LANES), lambda i, j: (i, j, 0)),
                  _const_spec(wk.shape), _const_spec(wv.shape)],
        out_specs=[pl.BlockSpec((1, tt, MLA_HEADS * QK_PAD), lambda i, j: (i, j, 0)),
                   pl.BlockSpec((1, tt, MLA_HEADS * V_PAD), lambda i, j: (i, j, 0))],
        out_shape=[jax.ShapeDtypeStruct((b, t, MLA_HEADS * QK_PAD), BF16),
                   jax.ShapeDtypeStruct((b, t, MLA_HEADS * V_PAD), BF16)],
        compiler_params=_cparams(("arbitrary", "arbitrary")),
        name="l1_kvup",
    )(ckv, kpe128, wk, wv)


def _qk(q, k):
    return lax.dot_general(q, k, (((1,), (1,)), ((), ())), preferred_element_type=F32)


def _attn_stream_kernel(q_ref, k_ref, v_ref, pat_ref, o_ref, q3, s0, s1, p0, p1, a0, a1, m_run, acc):
    t = q_ref.shape[1]
    qi = pl.program_id(2)
    q3[0] = q_ref[0]
    q3[1] = q_ref[0] + pat_ref[0]
    q3[2] = q_ref[0] + pat_ref[1]

    def scores(k):
        start = pl.multiple_of(jnp.minimum(k, qi) * t, t)
        which = jnp.where(k == qi, 1, jnp.where(k > qi, 2, 0))
        return _qk(q3[which], k_ref[0, pl.ds(start, t), :])

    def values(k):
        start = pl.multiple_of(jnp.clip(k, 0, qi) * t, t)
        return v_ref[0, pl.ds(start, t), :]

    def rescaled(alpha):
        return jnp.concatenate([alpha] * (V_PAD // LANES), axis=1) * acc[...]

    def half(k, s_cur, s_next, p_cur, p_prev, a_cur, a_prev, is_last=False):
        if not is_last:
            s_next[...] = scores(k + 1)
        acc[...] = rescaled(a_prev[...]) + jnp.dot(p_prev[...], values(k - 1), preferred_element_type=F32)
        m_old = m_run[...]
        m_new = jnp.maximum(m_old, jnp.max(s_cur[...], axis=-1, keepdims=True))
        a_cur[...] = jnp.exp2(m_old - m_new)
        m_run[...] = m_new
        p_cur[...] = jnp.exp2(s_cur[...] - jnp.concatenate([m_new] * (t // LANES), axis=1)).astype(p_cur.dtype)

    s0[...] = scores(0)
    p1[...] = jnp.zeros(p1.shape, p1.dtype)
    a1[...] = jnp.ones(a1.shape, F32)
    acc[...] = jnp.zeros(acc.shape, F32)
    m_run[...] = jnp.full(m_run.shape, NEG_BIG, F32)

    def body(j, carry):
        half(2 * j, s0, s1, p0, p1, a0, a1)
        half(2 * j + 1, s1, s0, p1, p0, a1, a0)
        return carry

    def finish(p_last, a_last):
        out = rescaled(a_last[...]) + jnp.dot(p_last[...], values(qi), preferred_element_type=F32)
        o_ref[0] = (out[:, :V_DIM] / out[:, V_DIM:V_DIM + 1]).astype(o_ref.dtype)

    pairs = (qi + 1) // 2
    lax.fori_loop(0, pairs, body, 0)
    odd = (qi + 1) % 2 == 1

    @pl.when(odd)
    def _():
        half(qi, s0, s1, p0, p1, a0, a1, is_last=True)
        finish(p0, a0)

    @pl.when(jnp.logical_not(odd))
    def _():
        finish(p1, a1)


def _attn_single_kernel(q_ref, k_ref, v_ref, o_ref, *, q_off):
    tq = q_ref.shape[1]
    tk = k_ref.shape[1]
    qc = (q_off + lax.broadcasted_iota(jnp.int32, (tq, tk), 0)) // CHUNK
    kc = lax.broadcasted_iota(jnp.int32, (tq, tk), 1) // CHUNK
    visible = kc <= qc
    for h in range(STEP_HEADS):
        qk_cols = slice(h * QK_PAD, (h + 1) * QK_PAD)
        s = _qk(q_ref[0, :, qk_cols], k_ref[0, :, qk_cols])
        s = jnp.where(visible, s, NEG_BIG)
        p = jnp.exp2(s - jnp.max(s, axis=-1, keepdims=True))
        out = jnp.dot(p.astype(BF16), v_ref[0, :, h * V_PAD:(h + 1) * V_PAD], preferred_element_type=F32)
        o_ref[0, :, h * V_DIM:(h + 1) * V_DIM] = (out[:, :V_DIM] / out[:, V_DIM:V_DIM + 1]).astype(o_ref.dtype)


def _attention(q, k, v):
    b, tq_all, _ = q.shape
    tk_all = k.shape[1]
    out_shape = jax.ShapeDtypeStruct((b, tq_all, MLA_HEADS * V_DIM), BF16)
    kv_specs = [pl.BlockSpec((1, tk_all, QK_PAD), lambda i, h, j: (i, 0, h)),
                pl.BlockSpec((1, tk_all, V_PAD), lambda i, h, j: (i, 0, h))]
    if tq_all == tk_all and tq_all % ATTN_TILE == 0:
        t = ATTN_TILE
        r = lax.broadcasted_iota(jnp.int32, (t, QK_PAD), 0) // CHUNK
        j = lax.broadcasted_iota(jnp.int32, (t, QK_PAD), 1) - MASK_COL
        in_cols = (j >= 0) & (j < t // CHUNK)
        pat = jnp.stack([(in_cols & (j > r)), in_cols]).astype(BF16)
        return pl.pallas_call(
            _attn_stream_kernel,
            grid=(b, MLA_HEADS, tq_all // t),
            in_specs=[pl.BlockSpec((1, t, QK_PAD), lambda i, h, j: (i, j, h))] + kv_specs
                     + [_const_spec((2, t, QK_PAD))],
            out_specs=pl.BlockSpec((1, t, V_DIM), lambda i, h, j: (i, j, h)),
            out_shape=out_shape,
            scratch_shapes=[pltpu.VMEM((3, t, QK_PAD), BF16),
                            pltpu.VMEM((t, t), F32), pltpu.VMEM((t, t), F32),
                            pltpu.VMEM((t, t), BF16), pltpu.VMEM((t, t), BF16),
                            pltpu.VMEM((t, LANES), F32), pltpu.VMEM((t, LANES), F32),
                            pltpu.VMEM((t, LANES), F32), pltpu.VMEM((t, V_PAD), F32)],
            compiler_params=_cparams(("arbitrary", "arbitrary", "arbitrary")),
            name="l1_attention",
        )(q, k, v, pat)
    assert tq_all <= ATTN_TILE, "history + long query block is not a shape this trunk step has"
    hg = STEP_HEADS
    return pl.pallas_call(
        functools.partial(_attn_single_kernel, q_off=tk_all - tq_all),
        grid=(b, MLA_HEADS // hg),
        in_specs=[pl.BlockSpec((1, tq_all, hg * QK_PAD), lambda i, h: (i, 0, h)),
                  pl.BlockSpec((1, tk_all, hg * QK_PAD), lambda i, h: (i, 0, h)),
                  pl.BlockSpec((1, tk_all, hg * V_PAD), lambda i, h: (i, 0, h))],
        out_specs=pl.BlockSpec((1, tq_all, hg * V_DIM), lambda i, h: (i, 0, h)),
        out_shape=out_shape,
        compiler_params=_cparams(("arbitrary", "arbitrary")),
        name="l1_attention_step",
    )(q, k, v)


def _prep_weights(w):
    p = {}
    p["rec_w_in"] = w["rec_w_in"][0].astype(BF16)
    p["rec_w_a"] = w["rec_w_a"][0].astype(BF16)
    p["rec_w_x"] = w["rec_w_x"][0].astype(BF16)
    p["pool_w"] = w["pool_w"][0].astype(BF16)
    p["rec_w_out"] = w["rec_w_out"][0].astype(BF16)
    p["att_w_dq"] = w["att_w_dq"][0].astype(BF16)
    wuq = w["att_w_uq"][0].reshape(Q_LORA, MLA_HEADS, QK_DIM)
    half = ROPE_DIM // 2
    zpad = jnp.zeros((Q_LORA, MLA_HEADS, QK_PAD - QK_DIM), F32)
    p["wq1"] = jnp.concatenate([wuq, zpad], axis=-1).reshape(Q_LORA, MLA_HEADS * QK_PAD).astype(BF16)
    swapped = jnp.concatenate([wuq[..., NOPE_DIM + half:], wuq[..., NOPE_DIM:NOPE_DIM + half],
                               jnp.zeros((Q_LORA, MLA_HEADS, LANES - ROPE_DIM), F32)], axis=-1)
    p["wq2"] = swapped.reshape(Q_LORA, MLA_HEADS * LANES).astype(BF16)
    wdkv = w["att_w_dkv"][0]
    zl = jnp.zeros((D_MODEL, LANES - ROPE_DIM), F32)
    p["wdkv"] = jnp.concatenate([wdkv, zl, wdkv[:, KV_LORA + half:], wdkv[:, KV_LORA:KV_LORA + half], zl],
                                axis=-1).astype(BF16)
    wukv = w["att_w_ukv"][0].reshape(KV_LORA, MLA_HEADS, NOPE_DIM + V_DIM)
    p["wk"] = wukv[..., :NOPE_DIM].reshape(KV_LORA, MLA_HEADS * NOPE_DIM).astype(BF16)
    p["wv"] = wukv[..., NOPE_DIM:].reshape(KV_LORA, MLA_HEADS * V_DIM).astype(BF16)
    p["att_w_out"] = w["att_w_out"][0].astype(BF16)
    p["peer_wqt"] = [w["peer_w_q"][l].T.astype(BF16) for l in range(2)]
    p["peer_keys"] = [w["peer_keys"][l].reshape(2 * PEER_HEADS, N_KEYS, HALF_KEY).astype(BF16) for l in range(2)]
    p["peer_u"] = [w["peer_u"][l].astype(BF16) for l in range(2)]
    p["peer_vt"] = [w["peer_v"][l].astype(BF16).reshape(N_EXPERTS // EXPERT_BLOCK, EXPERT_BLOCK, D_MODEL)
                    .transpose(0, 2, 1) for l in range(2)]
    return p


def _rope_tables(pos):
    half = ROPE_DIM // 2
    inv = ROPE_THETA ** (-(jnp.arange(half, dtype=F32) / half))
    ang = pos.astype(F32)[:, None] * inv[None, :]
    cos = jnp.cos(ang)
    sin = jnp.sin(ang)
    z = jnp.zeros((pos.shape[0], LANES - ROPE_DIM), F32)
    return jnp.concatenate([cos, cos, z], axis=-1), jnp.concatenate([-sin, sin, z], axis=-1)


def _peer(hn, x, g2, p, l, final_g=None):
    b, t, d = hn.shape
    e1, e2, th = _peer_route(hn.reshape(b * t, d), p["peer_wqt"][l], p["peer_keys"][l])
    return _peer_experts(hn, p["peer_u"][l], p["peer_vt"][l], e1, e2, th, x, g2, final_g)


def _trunk(x, mods, pos0, conv_prev, h_prev, pool_prev, ckv_prev, kpe_prev, w, p):
    b, t, d = x.shape
    row = lambda a: a.reshape(1, -1)
    sh1, sc1, g1, sh2, sc2, g2 = mods[0]
    z = _inproj(x, row(w["norm1_g"][0]), sh1, sc1, p["rec_w_in"])
    conv0 = jnp.pad(conv_prev, ((0, 0), (CONV_TAIL - (CONV_WIDTH - 1), 0), (0, 0)))
    pool0 = jnp.pad(pool_prev, ((0, 0), (POOL_TAIL - (POOL_MAX - 1), 0), (0, 0)))
    mix, h_last = _recmix(z, conv0, h_prev[:, None, :], pool0, w["rec_conv_w"][0], row(w["rec_conv_b"][0]),
                          p["rec_w_a"], row(w["rec_b_a"][0]), p["rec_w_x"], row(w["rec_b_x"][0]),
                          row(w["rec_lambda"][0]), p["pool_w"], row(w["pool_scale"][0]), pos0)
    new_conv = z[:, t - (CONV_WIDTH - 1):, LRU_WIDTH:2 * LRU_WIDTH]
    new_pool = z[:, t - (POOL_MAX - 1):, 2 * LRU_WIDTH:]
    x, hn = _proj_res(mix, p["rec_w_out"], x, g1, row(w["norm2_g"][0]), sh2, sc2)
    x = _peer(hn, x, g2, p, 0)
    sh1, sc1, g1, sh2, sc2, g2 = mods[1]
    pos = pos0 + jnp.arange(t, dtype=jnp.int32)
    cos, sin = _rope_tables(pos)
    q, ckv, kpe = _qkv(x, row(w["norm1_g"][1]), sh1, sc1, p["att_w_dq"], row(w["att_g_q"][0]), p["wq1"], p["wq2"],
                       p["wdkv"], row(w["att_g_kv"][0]), cos, sin)
    ckv_all = jnp.concatenate([ckv_prev, ckv], axis=1)
    kpe_all = jnp.concatenate([kpe_prev, kpe], axis=1)
    tk = kpe_all.shape[1]
    chunk_in_tile = (jnp.arange(tk, dtype=jnp.int32) % ATTN_TILE) // CHUNK
    mask_cols = jnp.where(chunk_in_tile[:, None] == jnp.arange(LANES - ROPE_DIM, dtype=jnp.int32)[None, :],
                          NEG_BIG, 0.0).astype(F32)
    kpe128 = jnp.concatenate([kpe_all, jnp.broadcast_to(mask_cols[None], (b, tk, LANES - ROPE_DIM))], axis=-1)
    k, v = _kvup(ckv_all, kpe128, p["wk"], p["wv"])
    o = _attention(q, k, v)
    x, hn = _proj_res(o, p["att_w_out"], x, g1, row(w["norm2_g"][1]), sh2, sc2)
    y = _peer(hn, x, g2, p, 1, row(w["final_g"]))
    return y, new_conv[None], h_last.reshape(1, b, LRU_WIDTH), new_pool[None], ckv[None], kpe[None]


def kernel(x_prompt, x_sample, c_prompt, c_sample, state_conv, state_lru_h, state_pool, cache_ckv, cache_kpe,
           ada_w, ada_b, norm1_g, norm2_g, rec_w_in, rec_conv_w, rec_conv_b, rec_w_a, rec_b_a, rec_w_x, rec_b_x,
           rec_lambda, pool_w, pool_scale, rec_w_out, att_w_dq, att_g_q, att_w_uq, att_w_dkv, att_g_kv, att_w_ukv,
           att_w_out, peer_w_q, peer_keys, peer_u, peer_v, final_g):
    w = dict(ada_w=ada_w, ada_b=ada_b, norm1_g=norm1_g, norm2_g=norm2_g, rec_w_in=rec_w_in, rec_conv_w=rec_conv_w,
             rec_conv_b=rec_conv_b, rec_w_a=rec_w_a, rec_b_a=rec_b_a, rec_w_x=rec_w_x, rec_b_x=rec_b_x,
             rec_lambda=rec_lambda, pool_w=pool_w, pool_scale=pool_scale, rec_w_out=rec_w_out, att_w_dq=att_w_dq,
             att_g_q=att_g_q, att_w_uq=att_w_uq, att_w_dkv=att_w_dkv, att_g_kv=att_g_kv, att_w_ukv=att_w_ukv,
             att_w_out=att_w_out, peer_w_q=peer_w_q, peer_keys=peer_keys, peer_u=peer_u, peer_v=peer_v,
             final_g=final_g)
    p = _prep_weights(w)
    bp, tp, d = x_prompt.shape
    bs, ts, _ = x_sample.shape
    past = cache_ckv.shape[2]
    depth = ada_w.shape[0]
    rows = bp + bs
    rows_pad = -(-rows // 8) * 8
    c_all = jnp.pad(jnp.concatenate([c_prompt, c_sample], axis=0), ((0, rows_pad - rows), (0, 0)))
    mod = _ada(c_all, ada_w, ada_b)

    def mods_of(lo, n):
        return [[mod[l, lo:lo + n, k * d:(k + 1) * d][:, None, :] for k in range(6)] for l in range(depth)]

    zeros = lambda *s: jnp.zeros(s, F32)
    out_p = _trunk(x_prompt, mods_of(0, bp), 0, zeros(bp, CONV_WIDTH - 1, LRU_WIDTH), zeros(bp, LRU_WIDTH),
                   zeros(bp, POOL_MAX - 1, POOL_WIDTH), zeros(bp, 0, KV_LORA), zeros(bp, 0, ROPE_DIM), w, p)
    out_s = _trunk(x_sample, mods_of(bp, bs), past, state_conv[0], state_lru_h[0], state_pool[0],
                   cache_ckv[0], cache_kpe[0], w, p)
    return (out_p[0], out_s[0], out_p[1], out_s[1], out_p[2], out_s[2],
            out_p[3], out_s[3], out_p[4], out_s[4], out_p[5], out_s[5])
```
